```python
import math
import jax, jax.numpy as jnp
from jax import lax
import numpy as np

D_MODEL = 1024
BATCH = 4
SEQ = 8192
DEPTH = 2

CHUNK = 64
D_MIX = D_MODEL
GLA_HEADS = 4
GLA_DK = 64
GLA_DV = 64
GLA_WIDTH = GLA_HEADS * GLA_DV
GLA_GATE_RANK = 16
GLA_GATE_TAU = 16.0
ATT_HEADS = 8
ATT_DH = 64
ATT_WIDTH = ATT_HEADS * ATT_DH
ATT_LEFT_CHUNKS = 8
ATT_BAND = ATT_LEFT_CHUNKS + 1
REL_CLIP = 256
LRU_WIDTH = D_MIX - GLA_WIDTH - ATT_WIDTH
LRU_BLOCKS = 4
LRU_BLOCK = LRU_WIDTH // LRU_BLOCKS
LRU_C = 8.0
CONV_WIDTH = 4
D_FF = -(-8 * D_MODEL // (3 * 256)) * 256
PLE_DIM = 256
DN_ALPHA = (2 * DEPTH) ** 0.25
DN_BETA = (8 * DEPTH) ** -0.25
LN_EPS = 1e-5
RMS_EPS = 1e-6
NEG_INF = -1e30
IN_SPLITS = [GLA_HEADS * GLA_DK, GLA_HEADS * GLA_DK, GLA_WIDTH, GLA_WIDTH, GLA_GATE_RANK,
             ATT_WIDTH, ATT_WIDTH, ATT_WIDTH, LRU_WIDTH, LRU_WIDTH]
D_IN = sum(IN_SPLITS)

kernel_name = "hymba_style_chunk_causal_hybrid_deepnorm"


def layer_norm(x, g, b):
    xf = x.astype(jnp.float32)
    mu = jnp.mean(xf, axis=-1, keepdims=True)
    var = jnp.mean(jnp.square(xf - mu), axis=-1, keepdims=True)
    return ((xf - mu) * lax.rsqrt(var + LN_EPS) * g.astype(jnp.float32) + b.astype(jnp.float32)).astype(x.dtype)


def gla_mixer(q, k, v, g, lr, w_lr_up, b_lr_up, norm_g):
    dtype = q.dtype
    B, L, _ = q.shape
    NC = L // CHUNK
    f32 = jnp.float32
    log_a = jax.nn.log_sigmoid(lr.astype(f32) @ w_lr_up.astype(f32) + b_lr_up.astype(f32)) / GLA_GATE_TAU
    shp_k = (B, NC, CHUNK, GLA_HEADS, GLA_DK)
    shp_v = (B, NC, CHUNK, GLA_HEADS, GLA_DV)
    log_a = log_a.reshape(shp_k)
    qc = q.astype(f32).reshape(shp_k) * (GLA_DK ** -0.5)
    kc = k.astype(f32).reshape(shp_k)
    vc = v.astype(f32).reshape(shp_v)
    bcum = jnp.cumsum(log_a, axis=2)
    ref = bcum[:, :, CHUNK // 2:CHUNK // 2 + 1]
    causal = jnp.tril(jnp.ones((CHUNK, CHUNK), dtype=bool))
    attn = jnp.einsum('bnihd,bnjhd->bnhij', qc * jnp.exp(bcum - ref), kc * jnp.exp(ref - bcum))
    attn = jnp.where(causal, attn, 0.0)
    o_intra = jnp.einsum('bnhij,bnjhv->bnihv', attn, vc)
    b_last = bcum[:, :, -1]
    u = jnp.einsum('bnjhd,bnjhv->bnhdv', kc * jnp.exp(b_last[:, :, None] - bcum), vc)
    decay = jnp.exp(b_last)

    def step(state, inp):
        d, uc = inp
        return d[..., None] * state + uc, state

    s0 = jnp.zeros((B, GLA_HEADS, GLA_DK, GLA_DV), f32)
    _, s_prev = lax.scan(step, s0, (jnp.swapaxes(decay, 0, 1), jnp.swapaxes(u, 0, 1)))
    s_prev = jnp.swapaxes(s_prev, 0, 1)
    o_inter = jnp.einsum('bnihd,bnhdv->bnihv', qc * jnp.exp(bcum), s_prev)
    o = (o_intra + o_inter).reshape(B, L, GLA_HEADS, GLA_DV)
    o = o * lax.rsqrt(jnp.mean(jnp.square(o), axis=-1, keepdims=True) + RMS_EPS) * norm_g.astype(f32)
    o = o.reshape(B, L, GLA_WIDTH) * jax.nn.silu(g.astype(f32))
    return o.astype(dtype)


def chunk_band_attention(q, k, v, rel_bias):
    B, L, _ = q.shape
    NC = L // CHUNK
    shp = (B, NC, CHUNK, ATT_HEADS, ATT_DH)
    qc = q.reshape(shp) * (ATT_DH ** -0.5)
    pad = ((0, 0), (ATT_LEFT_CHUNKS, 0), (0, 0), (0, 0), (0, 0))
    kp = jnp.pad(k.reshape(shp), pad)
    vp = jnp.pad(v.reshape(shp), pad)
    band_idx = jnp.arange(NC)[:, None] + jnp.arange(ATT_BAND)[None, :]
    q_pos = jnp.arange(CHUNK)
    k_pos = jnp.arange(ATT_BAND * CHUNK) - ATT_LEFT_CHUNKS * CHUNK
    rel_idx = jnp.clip(q_pos[:, None] - k_pos[None, :], -REL_CLIP, REL_CLIP) + REL_CLIP
    bias = rel_bias.astype(jnp.float32)[:, rel_idx]
    valid = (jnp.arange(NC)[:, None] - ATT_LEFT_CHUNKS + jnp.arange(ATT_BAND)[None, :]) >= 0
    valid = jnp.repeat(valid, CHUNK, axis=1)

    def one_sequence(args):
        qb, kb, vb = args
        kband = kb[band_idx].reshape(NC, ATT_BAND * CHUNK, ATT_HEADS, ATT_DH)
        vband = vb[band_idx].reshape(NC, ATT_BAND * CHUNK, ATT_HEADS, ATT_DH)
        s = jnp.einsum('nqhd,nkhd->hnqk', qb, kband).astype(jnp.float32) + bias[:, None]
        s = jnp.where(valid[None, :, None, :], s, NEG_INF)
        w = jax.nn.softmax(s, axis=-1).astype(vb.dtype)
        return jnp.einsum('hnqk,nkhd->nqhd', w, vband)

    o = lax.map(one_sequence, (qc, kp, vp))
    return o.reshape(B, L, ATT_WIDTH)


def rg_lru_mixer(xb, gb, conv_w, conv_b, w_a, b_a, w_i, b_i, lam):
    dtype = xb.dtype
    B, L, _ = xb.shape
    f32 = jnp.float32
    xc = lax.conv_general_dilated(xb, conv_w[:, None, :], (1,), [(CONV_WIDTH - 1, 0)],
                                  dimension_numbers=('NWC', 'WIO', 'NWC'),
                                  feature_group_count=LRU_WIDTH) + conv_b
    xblk = xc.reshape(B, L, LRU_BLOCKS, LRU_BLOCK)
    r = jax.nn.sigmoid(jnp.einsum('blgc,gcd->blgd', xblk, w_a).astype(f32)
                       + b_a.astype(f32).reshape(LRU_BLOCKS, LRU_BLOCK)).reshape(B, L, LRU_WIDTH)
    i = jax.nn.sigmoid(jnp.einsum('blgc,gcd->blgd', xblk, w_i).astype(f32)
                       + b_i.astype(f32).reshape(LRU_BLOCKS, LRU_BLOCK)).reshape(B, L, LRU_WIDTH)
    log_a = -LRU_C * r * jax.nn.softplus(-lam.astype(f32))
    a = jnp.exp(log_a)
    u = jnp.sqrt(-jnp.expm1(2.0 * log_a)) * (i * xc.astype(f32))

    def combine(c1, c2):
        a1, b1 = c1
        a2, b2 = c2
        return a1 * a2, a2 * b1 + b2

    _, h = lax.associative_scan(combine, (a, u), axis=1)
    return (h * jax.nn.gelu(gb.astype(f32))).astype(dtype)


def hybrid_layer(x, p_i, w_in, gla_w_lr_up, gla_b_lr_up, gla_norm_g, rel_bias,
                 lru_conv_w, lru_conv_b, lru_w_a, lru_b_a, lru_w_i, lru_b_i, lru_lambda,
                 w_out, ln1_g, ln1_b, w_ffn_gate, w_ffn_up, w_ffn_down,
                 w_ple_gate, w_ple_proj, ln2_g, ln2_b):
    h = x @ w_in
    offs = np.cumsum(IN_SPLITS)[:-1].tolist()
    (gq, gk, gv, gg, glr, aq, ak, av, lx, lg) = jnp.split(h, offs, axis=-1)
    y_gla = gla_mixer(gq, gk, gv, gg, glr, gla_w_lr_up, gla_b_lr_up, gla_norm_g)
    y_att = chunk_band_attention(aq, ak, av, rel_bias)
    y_lru = rg_lru_mixer(lx, lg, lru_conv_w, lru_conv_b, lru_w_a, lru_b_a, lru_w_i, lru_b_i, lru_lambda)
    mix = jnp.concatenate([y_gla, y_att, y_lru], axis=-1) @ w_out
    x = layer_norm(DN_ALPHA * x + mix, ln1_g, ln1_b)
    ffn = (jax.nn.silu(x @ w_ffn_gate) * (x @ w_ffn_up)) @ w_ffn_down
    ple = jax.nn.sigmoid(x @ w_ple_gate) * (p_i @ w_ple_proj)
    return layer_norm(DN_ALPHA * x + ffn + ple, ln2_g, ln2_b)


def setup_inputs(seed: int = 0) -> dict:
    key = jax.random.key(seed)
    ks = jax.random.split(key, 24)
    f32 = jnp.float32

    def nrm(k, shape, scale):
        return jax.random.normal(k, shape, f32) * scale

    u = jax.random.uniform(ks[11], (DEPTH, LRU_WIDTH), f32, 0.9, 0.999)
    s = u ** (1.0 / LRU_C)
    lru_lambda = jnp.log(s) - jnp.log1p(-s)
    return {
        "x": nrm(ks[0], (BATCH, SEQ, D_MODEL), 1.0),
        "p": nrm(ks[1], (DEPTH, BATCH, SEQ, PLE_DIM), 1.0),
        "w_in": nrm(ks[2], (DEPTH, D_MODEL, D_IN), D_MODEL ** -0.5),
        "gla_w_lr_up": nrm(ks[3], (DEPTH, GLA_GATE_RANK, GLA_HEADS * GLA_DK), GLA_GATE_RANK ** -0.5),
        "gla_b_lr_up": nrm(ks[4], (DEPTH, GLA_HEADS * GLA_DK), 0.1),
        "gla_norm_g": 1.0 + nrm(ks[5], (DEPTH, GLA_DV), 0.02),
        "rel_bias": nrm(ks[6], (ATT_HEADS, 2 * REL_CLIP + 1), 0.1),
        "lru_conv_w": nrm(ks[7], (DEPTH, CONV_WIDTH, LRU_WIDTH), CONV_WIDTH ** -0.5),
        "lru_conv_b": nrm(ks[8], (DEPTH, LRU_WIDTH), 0.01),
        "lru_w_a": nrm(ks[9], (DEPTH, LRU_BLOCKS, LRU_BLOCK, LRU_BLOCK), LRU_BLOCK ** -0.5),
        "lru_b_a": nrm(ks[10], (DEPTH, LRU_WIDTH), 0.01),
        "lru_w_i": nrm(ks[12], (DEPTH, LRU_BLOCKS, LRU_BLOCK, LRU_BLOCK), LRU_BLOCK ** -0.5),
        "lru_b_i": nrm(ks[13], (DEPTH, LRU_WIDTH), 0.01),
        "lru_lambda": lru_lambda,
        "w_out": nrm(ks[14], (DEPTH, D_MIX, D_MODEL), D_MIX ** -0.5 * DN_BETA),
        "ln1_g": 1.0 + nrm(ks[15], (DEPTH, D_MODEL), 0.02),
        "ln1_b": nrm(ks[16], (DEPTH, D_MODEL), 0.02),
        "w_ffn_gate": nrm(ks[17], (DEPTH, D_MODEL, D_FF), D_MODEL ** -0.5),
        "w_ffn_up": nrm(ks[18], (DEPTH, D_MODEL, D_FF), D_MODEL ** -0.5),
        "w_ffn_down": nrm(ks[19], (DEPTH, D_FF, D_MODEL), D_FF ** -0.5 * DN_BETA),
        "w_ple_gate": nrm(ks[20], (DEPTH, D_MODEL, D_MODEL), D_MODEL ** -0.5),
        "w_ple_proj": nrm(ks[21], (DEPTH, PLE_DIM, D_MODEL), PLE_DIM ** -0.5 * DN_BETA),
        "ln2_g": 1.0 + nrm(ks[22], (DEPTH, D_MODEL), 0.02),
        "ln2_b": nrm(ks[23], (DEPTH, D_MODEL), 0.02),
    }


def reference(x, p, w_in, gla_w_lr_up, gla_b_lr_up, gla_norm_g, rel_bias,
              lru_conv_w, lru_conv_b, lru_w_a, lru_b_a, lru_w_i, lru_b_i, lru_lambda,
              w_out, ln1_g, ln1_b, w_ffn_gate, w_ffn_up, w_ffn_down,
              w_ple_gate, w_ple_proj, ln2_g, ln2_b):
    for i in range(DEPTH):
        x = hybrid_layer(x, p[i], w_in[i], gla_w_lr_up[i], gla_b_lr_up[i], gla_norm_g[i], rel_bias,
                         lru_conv_w[i], lru_conv_b[i], lru_w_a[i], lru_b_a[i], lru_w_i[i], lru_b_i[i],
                         lru_lambda[i], w_out[i], ln1_g[i], ln1_b[i], w_ffn_gate[i], w_ffn_up[i],
                         w_ffn_down[i], w_ple_gate[i], w_ple_proj[i], ln2_g[i], ln2_b[i])
    return x
```

```python
import functools

import numpy as np
import jax
import jax.numpy as jnp
from jax import lax
from jax.experimental import pallas as pl
from jax.experimental.pallas import tpu as pltpu

F32 = jnp.float32
BF16 = jnp.bfloat16

D_MODEL = 1024
CHUNK = 64
GLA_HEADS = 4
GLA_DK = 64
GLA_WIDTH = 256
GLA_GATE_RANK = 16
GLA_GATE_TAU = 16.0
ATT_HEADS = 8
ATT_DH = 64
ATT_WIDTH = 512
ATT_LEFT_CHUNKS = 8
REL_CLIP = 256
LRU_WIDTH = 256
LRU_BLOCKS = 4
LRU_C = 8.0
CONV_WIDTH = 4
D_FF = 2816
PLE_DIM = 256
DEPTH = 2
DN_ALPHA = (2 * DEPTH) ** 0.25
LN_EPS = 1e-5
RMS_EPS = 1e-6
NEG_INF = -1e30

LANES = 128
SUBLANES = 8
VMEM_LIMIT_BYTES = 56 * 1024 * 1024

GLA_COLS = 4 * GLA_WIDTH
ATT_COLS = 3 * ATT_WIDTH
LRU_COLS = 2 * LRU_WIDTH
GLR_COLS = LANES
D_IN_PAD = GLA_COLS + ATT_COLS + LRU_COLS + GLR_COLS

ROW_TILE = 512
ATT_SUB = 256
ATT_WIN = ATT_SUB + ATT_LEFT_CHUNKS * CHUNK
FF_TILE = 256


def _dot(a, b):
    return jnp.dot(a, b, preferred_element_type=F32)


def _dot_nt(a, b):
    return lax.dot_general(a, b, (((1,), (1,)), ((), ())), preferred_element_type=F32)


def _dot_tn(a, b):
    return lax.dot_general(a, b, (((0,), (0,)), ((), ())), preferred_element_type=F32)


def _block_diag_mask(n, blk):
    r = lax.broadcasted_iota(jnp.int32, (n, n), 0) // blk
    c = lax.broadcasted_iota(jnp.int32, (n, n), 1) // blk
    return r == c


def _layer_norm(z, g, b):
    mu = jnp.mean(z, axis=-1, keepdims=True)
    zc = z - mu
    var = jnp.mean(zc * zc, axis=-1, keepdims=True)
    return zc * lax.rsqrt(var + LN_EPS) * g + b


def _softplus(x):
    return jnp.maximum(x, 0.0) + jnp.log1p(jnp.exp(-jnp.abs(x)))


def _in_proj_kernel(x_ref, w_ref, gla_ref, att_ref, lru_ref, glr_ref):
    xb = x_ref[...].astype(BF16)
    o0 = 0
    gla_ref[...] = _dot(xb, w_ref[:, o0:o0 + GLA_COLS])
    o0 += GLA_COLS
    att_ref[...] = _dot(xb, w_ref[:, o0:o0 + ATT_COLS]).astype(BF16)
    o0 += ATT_COLS
    lru_ref[...] = _dot(xb, w_ref[:, o0:o0 + LRU_COLS])
    o0 += LRU_COLS
    glr_ref[...] = _dot(xb, w_ref[:, o0:o0 + GLR_COLS])


def _in_proj(x2, w_perm):
    n = x2.shape[0]
    row = lambda c: pl.BlockSpec((ROW_TILE, c), lambda i: (i, 0))
    return pl.pallas_call(
        _in_proj_kernel,
        grid=(n // ROW_TILE,),
        in_specs=[row(D_MODEL),
                  pl.BlockSpec((D_MODEL, D_IN_PAD), lambda i: (0, 0), pipeline_mode=pl.Buffered(1))],
        out_specs=[row(GLA_COLS), row(ATT_COLS), row(LRU_COLS), row(GLR_COLS)],
        out_shape=[jax.ShapeDtypeStruct((n, GLA_COLS), F32),
                   jax.ShapeDtypeStruct((n, ATT_COLS), BF16),
                   jax.ShapeDtypeStruct((n, LRU_COLS), F32),
                   jax.ShapeDtypeStruct((n, GLR_COLS), F32)],
        compiler_params=pltpu.CompilerParams(
            dimension_semantics=("arbitrary",), vmem_limit_bytes=VMEM_LIMIT_BYTES),
        name="in_proj",
    )(x2, w_perm)


def _gla_kernel(h_ref, lr_ref, wup_ref, bup_ref, ng_ref, y_ref, loga_ref, state_ref):
    @pl.when(pl.program_id(1) == 0)
    def _():
        state_ref[...] = jnp.zeros_like(state_ref)

    z = _dot(lr_ref[...].astype(BF16), wup_ref[...]) + bup_ref[...]
    loga_ref[...] = -_softplus(-z) * (1.0 / GLA_GATE_TAU)

    bd = _block_diag_mask(GLA_WIDTH, GLA_DK)
    ri = lax.broadcasted_iota(jnp.int32, (CHUNK, CHUNK), 0)
    ci = lax.broadcasted_iota(jnp.int32, (CHUNK, CHUNK), 1)
    tri = jnp.where(ci <= ri, 1.0, 0.0).astype(BF16)
    ri4 = lax.broadcasted_iota(jnp.int32, (CHUNK, GLA_WIDTH), 0)
    ci4 = lax.broadcasted_iota(jnp.int32, (CHUNK, GLA_WIDTH), 1) % CHUNK
    causal4 = ci4 <= ri4
    group_mean = jnp.where(bd, 1.0 / GLA_DK, 0.0).astype(BF16)
    ng = ng_ref[...]

    def chunk_body(c, carry):
        rows = pl.ds(pl.multiple_of(c * CHUNK, CHUNK), CHUNK)
        la = loga_ref[rows, :]
        la0 = la.astype(BF16)
        r1 = la - la0.astype(F32)
        la1 = r1.astype(BF16)
        la2 = (r1 - la1.astype(F32)).astype(BF16)
        bcum = _dot(tri, la0) + _dot(tri, la1) + _dot(tri, la2)
        ref = bcum[CHUNK // 2:CHUNK // 2 + 1, :]
        b_last = bcum[CHUNK - 1:CHUNK, :]

        q = h_ref[rows, 0:GLA_WIDTH] * (GLA_DK ** -0.5)
        k = h_ref[rows, GLA_WIDTH:2 * GLA_WIDTH]
        v = h_ref[rows, 2 * GLA_WIDTH:3 * GLA_WIDTH]
        g = h_ref[rows, 3 * GLA_WIDTH:4 * GLA_WIDTH]

        qe = (q * jnp.exp(bcum - ref)).astype(BF16)
        ke = k * jnp.exp(ref - bcum)
        qd = (q * jnp.exp(bcum)).astype(BF16)
        kd = (k * jnp.exp(b_last - bcum)).astype(BF16)

        kstack = jnp.where(bd, jnp.concatenate([ke] * GLA_HEADS, axis=0), 0.0).astype(BF16)
        vstack = jnp.where(bd, jnp.concatenate([v] * GLA_HEADS, axis=0), 0.0).astype(BF16)
        attn = _dot_nt(qe, kstack)
        attn = jnp.where(causal4, attn, 0.0).astype(BF16)
        o = _dot(attn, vstack)

        state = state_ref[...]
        o = o + _dot_nt(qd, state.astype(BF16))
        ut = _dot_tn(v.astype(BF16), kd)
        state_ref[...] = state * jnp.exp(b_last) + jnp.where(bd, ut, 0.0)

        o2 = o * o
        o2h = o2.astype(BF16)
        o2l = (o2 - o2h.astype(F32)).astype(BF16)
        ms = _dot(o2h, group_mean) + _dot(o2l, group_mean)
        o = o * lax.rsqrt(ms + RMS_EPS) * ng
        y_ref[rows, :] = (o * (g * jax.nn.sigmoid(g))).astype(BF16)
        return carry

    lax.fori_loop(0, ROW_TILE // CHUNK, chunk_body, 0)


def _gla(gla_h, glr, wup_pad, bup, ng_row, batch):
    n = gla_h.shape[0]
    nt = n // batch // ROW_TILE
    row = lambda c: pl.BlockSpec((ROW_TILE, c), lambda b, i: (b * nt + i, 0))
    full = lambda r, c: pl.BlockSpec((r, c), lambda b, i: (0, 0))
    return pl.pallas_call(
        _gla_kernel,
        grid=(batch, nt),
        in_specs=[row(GLA_COLS), row(GLR_COLS), full(GLR_COLS, GLA_WIDTH),
                  full(1, GLA_WIDTH), full(1, GLA_WIDTH)],
        out_specs=row(GLA_WIDTH),
        out_shape=jax.ShapeDtypeStruct((n, GLA_WIDTH), BF16),
        scratch_shapes=[pltpu.VMEM((ROW_TILE, GLA_WIDTH), F32),
                        pltpu.VMEM((GLA_WIDTH, GLA_WIDTH), F32)],
        compiler_params=pltpu.CompilerParams(
            dimension_semantics=("arbitrary", "arbitrary"), vmem_limit_bytes=VMEM_LIMIT_BYTES),
        name="gla",
    )(gla_h, glr, wup_pad, bup, ng_row)


def _att_kernel(q_ref, kp_ref, kc_ref, vp_ref, vc_ref, bias_ref, y_ref, kbuf, vbuf):
    kbuf[0:ROW_TILE, :] = kp_ref[...]
    kbuf[ROW_TILE:2 * ROW_TILE, :] = kc_ref[...]
    vbuf[0:ROW_TILE, :] = vp_ref[...]
    vbuf[ROW_TILE:2 * ROW_TILE, :] = vc_ref[...]
    tile_row0 = pl.program_id(2) * ROW_TILE
    lane_head = lax.broadcasted_iota(jnp.int32, (ATT_SUB, LANES), 1) // ATT_DH
    col = lax.broadcasted_iota(jnp.int32, (1, ATT_WIN), 1)

    for j in range(ROW_TILE // ATT_SUB):
        q = q_ref[j * ATT_SUB:(j + 1) * ATT_SUB, :] * (ATT_DH ** -0.5)
        kw = kbuf[j * ATT_SUB:j * ATT_SUB + ATT_WIN, :]
        vw = vbuf[j * ATT_SUB:j * ATT_SUB + ATT_WIN, :]
        key_pos = tile_row0 + (j * ATT_SUB - ATT_LEFT_CHUNKS * CHUNK) + col
        start_mask = jnp.where(key_pos >= 0, 0.0, NEG_INF)
        acc = jnp.zeros((ATT_SUB, LANES), F32)
        for hh in range(LANES // ATT_DH):
            sel = lane_head == hh
            qm = jnp.where(sel, q, jnp.zeros_like(q))
            s = _dot_nt(qm, kw) + bias_ref[hh] + start_mask
            m = jnp.max(s, axis=-1, keepdims=True)
            p = jnp.exp(s - m)
            l = jnp.sum(p, axis=-1, keepdims=True)
            o = _dot(p.astype(BF16), vw) * (1.0 / l)
            acc = jnp.where(sel, o, acc)
        y_ref[j * ATT_SUB:(j + 1) * ATT_SUB, :] = acc.astype(BF16)


def _attention(att_h, bias_big, batch):
    n = att_h.shape[0]
    nt = n // batch // ROW_TILE
    pairs = ATT_WIDTH // LANES
    cur = lambda off: pl.BlockSpec((ROW_TILE, LANES), lambda p, b, i: (b * nt + i, off + p))
    prev = lambda off: pl.BlockSpec(
        (ROW_TILE, LANES), lambda p, b, i: (b * nt + jnp.maximum(i - 1, 0), off + p))
    return pl.pallas_call(
        _att_kernel,
        grid=(pairs, batch, nt),
        in_specs=[cur(0), prev(pairs), cur(pairs), prev(2 * pairs), cur(2 * pairs),
                  pl.BlockSpec((LANES // ATT_DH, ATT_SUB, ATT_WIN), lambda p, b, i: (p, 0, 0))],
        out_specs=pl.BlockSpec((ROW_TILE, LANES), lambda p, b, i: (b * nt + i, p)),
        out_shape=jax.ShapeDtypeStruct((n, ATT_WIDTH), BF16),
        scratch_shapes=[pltpu.VMEM((2 * ROW_TILE, LANES), BF16),
                        pltpu.VMEM((2 * ROW_TILE, LANES), BF16)],
        compiler_params=pltpu.CompilerParams(
            dimension_semantics=("arbitrary", "arbitrary", "arbitrary"),
            vmem_limit_bytes=VMEM_LIMIT_BYTES),
        name="band_attention",
    )(att_h, att_h, att_h, att_h, att_h, bias_big)


def _attention_bias(rel_bias):
    r = np.arange(ATT_SUB)
    w = np.arange(ATT_WIN)
    kk = w[None, :] - (r[:, None] // CHUNK) * CHUNK
    in_band = (kk >= 0) & (kk < (ATT_LEFT_CHUNKS + 1) * CHUNK)
    rel = (r[:, None] % CHUNK) - (kk - ATT_LEFT_CHUNKS * CHUNK)
    idx = np.clip(rel, -REL_CLIP, REL_CLIP) + REL_CLIP
    table = rel_bias.astype(F32)[:, idx]
    return jnp.where(in_band[None], table, NEG_INF)


def _shift_rows(x, d, fill, row_idx):
    return jnp.where(row_idx >= d, pltpu.roll(x, d, axis=0), fill)


def _lru_kernel(h_ref, cw_ref, cb_ref, wa_ref, ba_ref, wi_ref, bi_ref, lam_ref, y_ref,
                xbuf, hcarry):
    @pl.when(pl.program_id(1) == 0)
    def _():
        xbuf[0:SUBLANES, :] = jnp.zeros((SUBLANES, LRU_WIDTH), F32)
        hcarry[...] = jnp.zeros_like(hcarry)

    t = ROW_TILE
    xbuf[SUBLANES:SUBLANES + t, :] = h_ref[:, 0:LRU_WIDTH]
    xc = cb_ref[...] + jnp.zeros((t, LRU_WIDTH), F32)
    for w in range(CONV_WIDTH):
        off = SUBLANES - (CONV_WIDTH - 1) + w
        xc = xc + cw_ref[w:w + 1, :] * xbuf[off:off + t, :]
    xbuf[0:SUBLANES, :] = xbuf[t:t + SUBLANES, :]

    xcb = xc.astype(BF16)
    r = jax.nn.sigmoid(_dot(xcb, wa_ref[...]) + ba_ref[...])
    ig = jax.nn.sigmoid(_dot(xcb, wi_ref[...]) + bi_ref[...])
    log_a = (-LRU_C) * r * _softplus(-lam_ref[...])
    a = jnp.exp(log_a)
    th = jnp.tanh(log_a)
    u = jnp.sqrt(-2.0 * th / (1.0 - th)) * (ig * xc)

    row_idx = lax.broadcasted_iota(jnp.int32, (t, LRU_WIDTH), 0)
    d = 1
    while d < t:
        u = u + a * _shift_rows(u, d, 0.0, row_idx)
        a = a * _shift_rows(a, d, 1.0, row_idx)
        d *= 2
    h = u + a * hcarry[0:1, :]
    hcarry[...] = jnp.broadcast_to(h[t - 1:t, :], hcarry.shape)
    y_ref[...] = (h * jax.nn.gelu(h_ref[:, LRU_WIDTH:2 * LRU_WIDTH])).astype(BF16)


def _lru(lru_h, conv_w, conv_b, wa_bd, b_a, wi_bd, b_i, lam, batch):
    n = lru_h.shape[0]
    nt = n // batch // ROW_TILE
    row = lambda c: pl.BlockSpec((ROW_TILE, c), lambda b, i: (b * nt + i, 0))
    full = lambda r, c: pl.BlockSpec((r, c), lambda b, i: (0, 0))
    return pl.pallas_call(
        _lru_kernel,
        grid=(batch, nt),
        in_specs=[row(LRU_COLS), full(CONV_WIDTH, LRU_WIDTH), full(1, LRU_WIDTH),
                  full(LRU_WIDTH, LRU_WIDTH), full(1, LRU_WIDTH),
                  full(LRU_WIDTH, LRU_WIDTH), full(1, LRU_WIDTH), full(1, LRU_WIDTH)],
        out_specs=row(LRU_WIDTH),
        out_shape=jax.ShapeDtypeStruct((n, LRU_WIDTH), BF16),
        scratch_shapes=[pltpu.VMEM((ROW_TILE + SUBLANES, LRU_WIDTH), F32),
                        pltpu.VMEM((SUBLANES, LRU_WIDTH), F32)],
        compiler_params=pltpu.CompilerParams(
            dimension_semantics=("arbitrary", "arbitrary"), vmem_limit_bytes=VMEM_LIMIT_BYTES),
        name="rg_lru",
    )(lru_h, conv_w, conv_b, wa_bd, b_a, wi_bd, b_i, lam)


def _post_kernel(yg_ref, ya_ref, yl_ref, x_ref, p_ref, wo_ref, g1_ref, b1_ref,
                 wg_ref, wu_ref, wd_ref, wpg_ref, wpp_ref, g2_ref, b2_ref, o_ref, act_ref):
    mix = (_dot(yg_ref[...], wo_ref[0:GLA_WIDTH, :])
           + _dot(ya_ref[...], wo_ref[GLA_WIDTH:GLA_WIDTH + ATT_WIDTH, :])
           + _dot(yl_ref[...], wo_ref[GLA_WIDTH + ATT_WIDTH:D_MODEL, :]))
    x1 = _layer_norm(DN_ALPHA * x_ref[...] + mix, g1_ref[...], b1_ref[...])
    xb = x1.astype(BF16)
    for c in range(D_FF // FF_TILE):
        cols = slice(c * FF_TILE, (c + 1) * FF_TILE)
        gate = _dot(xb, wg_ref[:, cols])
        up = _dot(xb, wu_ref[:, cols])
        act_ref[:, cols] = (gate * jax.nn.sigmoid(gate) * up).astype(BF16)
    ffn = _dot(act_ref[...], wd_ref[...])
    ple = jax.nn.sigmoid(_dot(xb, wpg_ref[...])) * _dot(p_ref[...].astype(BF16), wpp_ref[...])
    o_ref[...] = _layer_norm(DN_ALPHA * x1 + ffn + ple, g2_ref[...], b2_ref[...])


def _post(y_gla, y_att, y_lru, x2, p2, wo, g1, b1, wg, wu, wd, wpg, wpp, g2, b2):
    n = x2.shape[0]
    row = lambda c: pl.BlockSpec((ROW_TILE, c), lambda i: (i, 0))
    full = lambda r, c: pl.BlockSpec((r, c), lambda i: (0, 0), pipeline_mode=pl.Buffered(1))
    return pl.pallas_call(
        _post_kernel,
        grid=(n // ROW_TILE,),
        in_specs=[row(GLA_WIDTH), row(ATT_WIDTH), row(LRU_WIDTH), row(D_MODEL), row(PLE_DIM),
                  full(D_MODEL, D_MODEL), full(1, D_MODEL), full(1, D_MODEL),
                  full(D_MODEL, D_FF), full(D_MODEL, D_FF), full(D_FF, D_MODEL),
                  full(D_MODEL, D_MODEL), full(PLE_DIM, D_MODEL),
                  full(1, D_MODEL), full(1, D_MODEL)],
        out_specs=row(D_MODEL),
        out_shape=jax.ShapeDtypeStruct((n, D_MODEL), F32),
        scratch_shapes=[pltpu.VMEM((ROW_TILE, D_FF), BF16)],
        compiler_params=pltpu.CompilerParams(
            dimension_semantics=("arbitrary",), vmem_limit_bytes=VMEM_LIMIT_BYTES),
        name="out_proj_ffn",
    )(y_gla, y_att, y_lru, x2, p2, wo, g1, b1, wg, wu, wd, wpg, wpp, g2, b2)


def _block_diag(w):
    g, c, d = w.shape
    eye = jnp.eye(g, dtype=w.dtype)
    return (eye[:, None, :, None] * w[:, :, None, :]).reshape(g * c, g * d)


def kernel(x, p, w_in, gla_w_lr_up, gla_b_lr_up, gla_norm_g, rel_bias, lru_conv_w, lru_conv_b,
           lru_w_a, lru_b_a, lru_w_i, lru_b_i, lru_lambda, w_out, ln1_g, ln1_b, w_ffn_gate,
           w_ffn_up, w_ffn_down, w_ple_gate, w_ple_proj, ln2_g, ln2_b):
    batch, seq, _ = x.shape
    n = batch * seq
    assert seq % ROW_TILE == 0
    x2 = x.reshape(n, D_MODEL)
    bias_big = _attention_bias(rel_bias)
    glr_lo = GLA_COLS
    glr_hi = GLA_COLS + GLA_GATE_RANK
    row = lambda a: a.reshape(1, -1).astype(F32)

    for i in range(DEPTH):
        w = w_in[i]
        w_perm = jnp.concatenate(
            [w[:, :glr_lo], w[:, glr_hi:], w[:, glr_lo:glr_hi],
             jnp.zeros((D_MODEL, GLR_COLS - GLA_GATE_RANK), w.dtype)], axis=1).astype(BF16)
        wup_pad = jnp.concatenate(
            [gla_w_lr_up[i], jnp.zeros((GLR_COLS - GLA_GATE_RANK, GLA_WIDTH), F32)],
            axis=0).astype(BF16)

        gla_h, att_h, lru_h, glr = _in_proj(x2, w_perm)
        y_gla = _gla(gla_h, glr, wup_pad, row(gla_b_lr_up[i]),
                     row(jnp.tile(gla_norm_g[i], GLA_HEADS)), batch)
        y_att = _attention(att_h, bias_big, batch)
        y_lru = _lru(lru_h, lru_conv_w[i].astype(F32), row(lru_conv_b[i]),
                     _block_diag(lru_w_a[i]).astype(BF16), row(lru_b_a[i]),
                     _block_diag(lru_w_i[i]).astype(BF16), row(lru_b_i[i]),
                     row(lru_lambda[i]), batch)
        x2 = _post(y_gla, y_att, y_lru, x2, p[i].reshape(n, PLE_DIM),
                   w_out[i].astype(BF16), row(ln1_g[i]), row(ln1_b[i]),
                   w_ffn_gate[i].astype(BF16), w_ffn_up[i].astype(BF16),
                   w_ffn_down[i].astype(BF16), w_ple_gate[i].astype(BF16),
                   w_ple_proj[i].astype(BF16), row(ln2_g[i]), row(ln2_b[i]))
    return x2.reshape(batch, seq, D_MODEL)
```

```python
import math

import numpy as np
import jax
import jax.numpy as jnp
from jax import lax
from jax.experimental import pallas as pl
from jax.experimental.pallas import tpu as pltpu

F32 = jnp.float32
BF16 = jnp.bfloat16

D_MODEL = 1024
CHUNK = 64
GLA_HEADS = 4
GLA_DK = 64
GLA_WIDTH = 256
GLA_GATE_RANK = 16
GLA_GATE_TAU = 16.0
ATT_HEADS = 8
ATT_DH = 64
ATT_WIDTH = 512
ATT_LEFT_CHUNKS = 8
REL_CLIP = 256
LRU_WIDTH = 256
LRU_BLOCKS = 4
LRU_C = 8.0
CONV_WIDTH = 4
D_FF = 2816
PLE_DIM = 256
DEPTH = 2
DN_ALPHA = (2 * DEPTH) ** 0.25
LN_EPS = 1e-5
RMS_EPS = 1e-6
NEG_INF = -1e30
LOG2_E = math.log2(math.e)

LANES = 128
SUBLANES = 8
VMEM_LIMIT_BYTES = 56 * 1024 * 1024

GLA_COLS = 4 * GLA_WIDTH
ATT_COLS = 3 * ATT_WIDTH
LRU_COLS = 2 * LRU_WIDTH
GLR_COLS = LANES
D_IN_PAD = GLA_COLS + ATT_COLS + LRU_COLS + GLR_COLS

ROW_TILE = 512
ATT_SUB = 256
ATT_WIN = ATT_SUB + ATT_LEFT_CHUNKS * CHUNK
FF_TILE = 256


def _dot(a, b):
    return jnp.dot(a, b, preferred_element_type=F32)


def _dot_nt(a, b):
    return lax.dot_general(a, b, (((1,), (1,)), ((), ())), preferred_element_type=F32)


def _dot_tn(a, b):
    return lax.dot_general(a, b, (((0,), (0,)), ((), ())), preferred_element_type=F32)


def _block_diag_mask(n, blk):
    r = lax.broadcasted_iota(jnp.int32, (n, n), 0) // blk
    c = lax.broadcasted_iota(jnp.int32, (n, n), 1) // blk
    return r == c


def _layer_norm(z, g, b):
    mu = jnp.mean(z, axis=-1, keepdims=True)
    zc = z - mu
    var = jnp.mean(zc * zc, axis=-1, keepdims=True)
    return zc * lax.rsqrt(var + LN_EPS) * g + b


def _softplus(x):
    return jnp.maximum(x, 0.0) + jnp.log1p(jnp.exp(-jnp.abs(x)))


def _split3(x):
    x0 = x.astype(BF16)
    r1 = x - x0.astype(F32)
    x1 = r1.astype(BF16)
    x2 = (r1 - x1.astype(F32)).astype(BF16)
    return x0, x1, x2


def _in_proj_kernel(x_ref, w_ref, gla_ref, att_ref, lru_ref, glr_ref):
    xb = x_ref[...].astype(BF16)
    o0 = 0
    gla_ref[...] = _dot(xb, w_ref[:, o0:o0 + GLA_COLS])
    o0 += GLA_COLS
    att_ref[...] = _dot(xb, w_ref[:, o0:o0 + ATT_COLS]).astype(BF16)
    o0 += ATT_COLS
    lru_ref[...] = _dot(xb, w_ref[:, o0:o0 + LRU_COLS])
    o0 += LRU_COLS
    glr_ref[...] = _dot(xb, w_ref[:, o0:o0 + GLR_COLS])


def _in_proj(x2, w_perm):
    n = x2.shape[0]
    row = lambda c: pl.BlockSpec((ROW_TILE, c), lambda i: (i, 0))
    return pl.pallas_call(
        _in_proj_kernel,
        grid=(n // ROW_TILE,),
        in_specs=[row(D_MODEL),
                  pl.BlockSpec((D_MODEL, D_IN_PAD), lambda i: (0, 0), pipeline_mode=pl.Buffered(1))],
        out_specs=[row(GLA_COLS), row(ATT_COLS), row(LRU_COLS), row(GLR_COLS)],
        out_shape=[jax.ShapeDtypeStruct((n, GLA_COLS), F32),
                   jax.ShapeDtypeStruct((n, ATT_COLS), BF16),
                   jax.ShapeDtypeStruct((n, LRU_COLS), F32),
                   jax.ShapeDtypeStruct((n, GLR_COLS), F32)],
        compiler_params=pltpu.CompilerParams(
            dimension_semantics=("arbitrary",), vmem_limit_bytes=VMEM_LIMIT_BYTES),
        name="in_proj",
    )(x2, w_perm)


def _gla_kernel(h_ref, lr_ref, wup_ref, bup_ref, ng_ref, bdm_ref, y_ref, o_ref, state_ref):
    @pl.when(pl.program_id(1) == 0)
    def _():
        state_ref[...] = jnp.zeros_like(state_ref)

    t = ROW_TILE
    z = _dot(lr_ref[...].astype(BF16), wup_ref[...]) + bup_ref[...]
    la = -_softplus(-z) * (1.0 / GLA_GATE_TAU)
    la_hi = la.astype(BF16)
    la_lo = (la - la_hi.astype(F32)).astype(BF16)
    ri = lax.broadcasted_iota(jnp.int32, (CHUNK, CHUNK), 0)
    ci = lax.broadcasted_iota(jnp.int32, (CHUNK, CHUNK), 1)
    tri = jnp.where(ci <= ri, 1.0, 0.0).astype(BF16)
    bcum, refb, lastb = [], [], []
    for c in range(t // CHUNK):
        rows = slice(c * CHUNK, (c + 1) * CHUNK)
        bc = _dot(tri, la_hi[rows]) + _dot(tri, la_lo[rows])
        bcum.append(bc)
        refb.append(jnp.broadcast_to(bc[CHUNK // 2:CHUNK // 2 + 1], bc.shape))
        lastb.append(jnp.broadcast_to(bc[CHUNK - 1:CHUNK], bc.shape))
    bcum = jnp.concatenate(bcum, axis=0)
    refb = jnp.concatenate(refb, axis=0)
    lastb = jnp.concatenate(lastb, axis=0)

    q = h_ref[:, 0:GLA_WIDTH] * (GLA_DK ** -0.5)
    k = h_ref[:, GLA_WIDTH:2 * GLA_WIDTH]
    qe = (q * jnp.exp(bcum - refb)).astype(BF16)
    ke = (k * jnp.exp(refb - bcum)).astype(BF16)
    qd = (q * jnp.exp(bcum)).astype(BF16)
    kd = (k * jnp.exp(lastb - bcum)).astype(BF16)
    vb = h_ref[:, 2 * GLA_WIDTH:3 * GLA_WIDTH].astype(BF16)

    bd = _block_diag_mask(GLA_WIDTH, GLA_DK)
    bd16 = bdm_ref[...] > 0
    zero16 = jnp.zeros((GLA_WIDTH, GLA_WIDTH), BF16)
    ri4 = lax.broadcasted_iota(jnp.int32, (CHUNK, GLA_WIDTH), 0)
    ci4 = lax.broadcasted_iota(jnp.int32, (CHUNK, GLA_WIDTH), 1) % CHUNK
    causal4 = ci4 <= ri4

    state = state_ref[...]
    for c in range(t // CHUNK):
        rows = slice(c * CHUNK, (c + 1) * CHUNK)
        kstack = jnp.where(bd16, jnp.concatenate([ke[rows]] * GLA_HEADS, axis=0), zero16)
        vstack = jnp.where(bd16, jnp.concatenate([vb[rows]] * GLA_HEADS, axis=0), zero16)
        attn = _dot_nt(qe[rows], kstack)
        attn = jnp.where(causal4, attn, 0.0).astype(BF16)
        o_ref[rows, :] = _dot(attn, vstack) + _dot_nt(qd[rows], state.astype(BF16))
        ut = _dot_tn(vb[rows], kd[rows])
        decay = jnp.exp(lastb[c * CHUNK:c * CHUNK + 1, :])
        state = state * decay + jnp.where(bd, ut, 0.0)
    state_ref[...] = state

    o = o_ref[...]
    group_mean = jnp.where(bd16, jnp.full_like(zero16, 1.0 / GLA_DK), zero16)
    ms = _dot((o * o).astype(BF16), group_mean)
    g = h_ref[:, 3 * GLA_WIDTH:4 * GLA_WIDTH]
    y_ref[...] = (o * lax.rsqrt(ms + RMS_EPS) * ng_ref[...] * (g * jax.nn.sigmoid(g))).astype(BF16)


def _gla(gla_h, glr, wup_pad, bup, ng_row, batch):
    n = gla_h.shape[0]
    nt = n // batch // ROW_TILE
    d = np.arange(GLA_WIDTH) // GLA_DK
    bdm = jnp.asarray((d[:, None] == d[None, :]).astype(np.float32), BF16)
    row = lambda c: pl.BlockSpec((ROW_TILE, c), lambda b, i: (b * nt + i, 0))
    full = lambda r, c: pl.BlockSpec((r, c), lambda b, i: (0, 0))
    return pl.pallas_call(
        _gla_kernel,
        grid=(batch, nt),
        in_specs=[row(GLA_COLS), row(GLR_COLS), full(GLR_COLS, GLA_WIDTH),
                  full(1, GLA_WIDTH), full(1, GLA_WIDTH), full(GLA_WIDTH, GLA_WIDTH)],
        out_specs=row(GLA_WIDTH),
        out_shape=jax.ShapeDtypeStruct((n, GLA_WIDTH), BF16),
        scratch_shapes=[pltpu.VMEM((ROW_TILE, GLA_WIDTH), F32),
                        pltpu.VMEM((GLA_WIDTH, GLA_WIDTH), F32)],
        compiler_params=pltpu.CompilerParams(
            dimension_semantics=("arbitrary", "arbitrary"), vmem_limit_bytes=VMEM_LIMIT_BYTES),
        name="gla",
    )(gla_h, glr, wup_pad, bup, ng_row, bdm)


def _att_kernel(q_ref, kw_ref, vw_ref, bias_ref, hm_ref, y_ref):
    lane_head0 = lax.broadcasted_iota(jnp.int32, (ATT_SUB, LANES), 1) < ATT_DH

    def body(first):
        n_keys = ROW_TILE if first else 2 * ROW_TILE
        head0 = hm_ref[0:n_keys, :] > 0
        v = vw_ref[0:n_keys, :]
        one16 = jnp.ones_like(v)
        vh = (jnp.where(head0, v, one16), jnp.where(head0, one16, v))
        q = q_ref[...]
        zero16 = jnp.zeros_like(q)
        qh = (jnp.where(head0[0:ROW_TILE], q, zero16), jnp.where(head0[0:ROW_TILE], zero16, q))
        for j in range(ROW_TILE // ATT_SUB):
            rows = slice(j * ATT_SUB, (j + 1) * ATT_SUB)
            if first:
                win = slice(0, (j + 1) * ATT_SUB)
            else:
                win = slice(j * ATT_SUB, j * ATT_SUB + ATT_WIN)
            n_win = win.stop - win.start
            kw = kw_ref[win, :]
            outs = []
            for hh in range(LANES // ATT_DH):
                s = _dot_nt(qh[hh][rows], kw) + bias_ref[hh, :, ATT_WIN - n_win:ATT_WIN]
                m = jnp.max(s, axis=-1, keepdims=True)
                p = jnp.exp2(s - m).astype(BF16)
                of = _dot(p, vh[hh][win])
                outs.append(of / pltpu.roll(of, ATT_DH, axis=1))
            y_ref[rows, :] = jnp.where(lane_head0, outs[0], outs[1]).astype(BF16)

    @pl.when(pl.program_id(2) == 0)
    def _():
        body(True)

    @pl.when(pl.program_id(2) > 0)
    def _():
        body(False)


def _attention(att_h, bias_big, batch):
    n = att_h.shape[0]
    seq = n // batch
    att3 = att_h.reshape(batch, seq, ATT_COLS)
    pairs = ATT_WIDTH // LANES
    head0 = jnp.asarray(
        np.broadcast_to(np.arange(LANES) < ATT_DH, (2 * ROW_TILE, LANES)).astype(np.float32), BF16)
    window = lambda off: pl.BlockSpec(
        (None, pl.Element(2 * ROW_TILE), pl.Element(LANES)),
        lambda p, b, i: (b, jnp.maximum(i - 1, 0) * ROW_TILE, (off + p) * LANES))
    y = pl.pallas_call(
        _att_kernel,
        grid=(pairs, batch, seq // ROW_TILE),
        in_specs=[pl.BlockSpec((None, ROW_TILE, LANES), lambda p, b, i: (b, i, p)),
                  window(pairs), window(2 * pairs),
                  pl.BlockSpec((LANES // ATT_DH, ATT_SUB, ATT_WIN), lambda p, b, i: (p, 0, 0)),
                  pl.BlockSpec((2 * ROW_TILE, LANES), lambda p, b, i: (0, 0))],
        out_specs=pl.BlockSpec((None, ROW_TILE, LANES), lambda p, b, i: (b, i, p)),
        out_shape=jax.ShapeDtypeStruct((batch, seq, ATT_WIDTH), BF16),
        compiler_params=pltpu.CompilerParams(
            dimension_semantics=("arbitrary", "arbitrary", "arbitrary"),
            vmem_limit_bytes=VMEM_LIMIT_BYTES),
        name="band_attention",
    )(att3, att3, att3, bias_big, head0)
    return y.reshape(n, ATT_WIDTH)


def _attention_bias(rel_bias):
    r = np.arange(ATT_SUB)
    w = np.arange(ATT_WIN)
    kk = w[None, :] - (r[:, None] // CHUNK) * CHUNK
    in_band = (kk >= 0) & (kk < (ATT_LEFT_CHUNKS + 1) * CHUNK)
    n_diag = ATT_SUB + ATT_WIN - 1
    rel = ATT_LEFT_CHUNKS * CHUNK + (ATT_SUB - 1) - np.arange(n_diag)
    idx = np.clip(rel, -REL_CLIP, REL_CLIP) + REL_CLIP
    diag = jnp.pad(rel_bias.astype(F32)[:, idx] * LOG2_E, ((0, 0), (0, 1)))
    skew = jnp.tile(diag, (1, ATT_SUB))[:, :ATT_SUB * n_diag].reshape(-1, ATT_SUB, n_diag)
    table = skew[:, :, ATT_SUB - 1:ATT_SUB - 1 + ATT_WIN]
    return jnp.where(in_band[None], table, NEG_INF)


def _shift_rows(x, d, fill, row_idx):
    return jnp.where(row_idx >= d, pltpu.roll(x, d, axis=0), fill)


def _lru_kernel(h_ref, cw_ref, cb_ref, wa_ref, ba_ref, wi_ref, bi_ref, lam_ref, y_ref,
                xbuf, hcarry):
    @pl.when(pl.program_id(1) == 0)
    def _():
        xbuf[0:SUBLANES, :] = jnp.zeros((SUBLANES, LRU_WIDTH), F32)
        hcarry[...] = jnp.zeros_like(hcarry)

    t = ROW_TILE
    xbuf[SUBLANES:SUBLANES + t, :] = h_ref[:, 0:LRU_WIDTH]
    xc = cb_ref[...] + jnp.zeros((t, LRU_WIDTH), F32)
    for w in range(CONV_WIDTH):
        off = SUBLANES - (CONV_WIDTH - 1) + w
        xc = xc + cw_ref[w:w + 1, :] * xbuf[off:off + t, :]
    xbuf[0:SUBLANES, :] = xbuf[t:t + SUBLANES, :]

    xcb = xc.astype(BF16)
    r = jax.nn.sigmoid(_dot(xcb, wa_ref[...]) + ba_ref[...])
    ig = jax.nn.sigmoid(_dot(xcb, wi_ref[...]) + bi_ref[...])
    log_a = (-LRU_C) * r * _softplus(-lam_ref[...])
    a = jnp.exp(log_a)
    th = jnp.tanh(log_a)
    u = jnp.sqrt(-2.0 * th / (1.0 - th)) * (ig * xc)

    row_idx = lax.broadcasted_iota(jnp.int32, (t, LRU_WIDTH), 0)
    d = 1
    while d < t:
        u = u + a * _shift_rows(u, d, 0.0, row_idx)
        a = a * _shift_rows(a, d, 1.0, row_idx)
        d *= 2
    h = u + a * hcarry[0:1, :]
    hcarry[...] = jnp.broadcast_to(h[t - 1:t, :], hcarry.shape)
    y_ref[...] = (h * jax.nn.gelu(h_ref[:, LRU_WIDTH:2 * LRU_WIDTH])).astype(BF16)


def _lru(lru_h, conv_w, conv_b, wa_bd, b_a, wi_bd, b_i, lam, batch):
    n = lru_h.shape[0]
    nt = n // batch // ROW_TILE
    row = lambda c: pl.BlockSpec((ROW_TILE, c), lambda b, i: (b * nt + i, 0))
    full = lambda r, c: pl.BlockSpec((r, c), lambda b, i: (0, 0))
    return pl.pallas_call(
        _lru_kernel,
        grid=(batch, nt),
        in_specs=[row(LRU_COLS), full(CONV_WIDTH, LRU_WIDTH), full(1, LRU_WIDTH),
                  full(LRU_WIDTH, LRU_WIDTH), full(1, LRU_WIDTH),
                  full(LRU_WIDTH, LRU_WIDTH), full(1, LRU_WIDTH), full(1, LRU_WIDTH)],
        out_specs=row(LRU_WIDTH),
        out_shape=jax.ShapeDtypeStruct((n, LRU_WIDTH), BF16),
        scratch_shapes=[pltpu.VMEM((ROW_TILE + SUBLANES, LRU_WIDTH), F32),
                        pltpu.VMEM((SUBLANES, LRU_WIDTH), F32)],
        compiler_params=pltpu.CompilerParams(
            dimension_semantics=("arbitrary", "arbitrary"), vmem_limit_bytes=VMEM_LIMIT_BYTES),
        name="rg_lru",
    )(lru_h, conv_w, conv_b, wa_bd, b_a, wi_bd, b_i, lam)


def _post_kernel(yg_ref, ya_ref, yl_ref, x_ref, p_ref, wo_ref, g1_ref, b1_ref,
                 wg_ref, wu_ref, wd_ref, wpg_ref, wpp_ref, g2_ref, b2_ref, o_ref, act_ref):
    mix = (_dot(yg_ref[...], wo_ref[0:GLA_WIDTH, :])
           + _dot(ya_ref[...], wo_ref[GLA_WIDTH:GLA_WIDTH + ATT_WIDTH, :])
           + _dot(yl_ref[...], wo_ref[GLA_WIDTH + ATT_WIDTH:D_MODEL, :]))
    x1 = _layer_norm(DN_ALPHA * x_ref[...] + mix, g1_ref[...], b1_ref[...])
    xb = x1.astype(BF16)
    for c in range(D_FF // FF_TILE):
        cols = slice(c * FF_TILE, (c + 1) * FF_TILE)
        gate = _dot(xb, wg_ref[:, cols])
        up = _dot(xb, wu_ref[:, cols])
        act_ref[:, cols] = (gate * jax.nn.sigmoid(gate) * up).astype(BF16)
    ffn = _dot(act_ref[...], wd_ref[...])
    ple = jax.nn.sigmoid(_dot(xb, wpg_ref[...])) * _dot(p_ref[...].astype(BF16), wpp_ref[...])
    o_ref[...] = _layer_norm(DN_ALPHA * x1 + ffn + ple, g2_ref[...], b2_ref[...])


def _post(y_gla, y_att, y_lru, x2, p2, wo, g1, b1, wg, wu, wd, wpg, wpp, g2, b2):
    n = x2.shape[0]
    row = lambda c: pl.BlockSpec((ROW_TILE, c), lambda i: (i, 0))
    full = lambda r, c: pl.BlockSpec((r, c), lambda i: (0, 0), pipeline_mode=pl.Buffered(1))
    return pl.pallas_call(
        _post_kernel,
        grid=(n // ROW_TILE,),
        in_specs=[row(GLA_WIDTH), row(ATT_WIDTH), row(LRU_WIDTH), row(D_MODEL), row(PLE_DIM),
                  full(D_MODEL, D_MODEL), full(1, D_MODEL), full(1, D_MODEL),
                  full(D_MODEL, D_FF), full(D_MODEL, D_FF), full(D_FF, D_MODEL),
                  full(D_MODEL, D_MODEL), full(PLE_DIM, D_MODEL),
                  full(1, D_MODEL), full(1, D_MODEL)],
        out_specs=row(D_MODEL),
        out_shape=jax.ShapeDtypeStruct((n, D_MODEL), F32),
        scratch_shapes=[pltpu.VMEM((ROW_TILE, D_FF), BF16)],
        compiler_params=pltpu.CompilerParams(
            dimension_semantics=("arbitrary",), vmem_limit_bytes=VMEM_LIMIT_BYTES),
        name="out_proj_ffn",
    )(y_gla, y_att, y_lru, x2, p2, wo, g1, b1, wg, wu, wd, wpg, wpp, g2, b2)


def _block_diag(w):
    g, c, d = w.shape
    eye = jnp.eye(g, dtype=w.dtype)
    return (eye[:, None, :, None] * w[:, :, None, :]).reshape(g * c, g * d)


def _permute_w_in(w):
    glr_lo = GLA_COLS
    glr_hi = GLA_COLS + GLA_GATE_RANK
    aq_hi = glr_hi + ATT_WIDTH
    return jnp.concatenate(
        [w[:, :glr_lo], w[:, glr_hi:aq_hi] * (ATT_DH ** -0.5 * LOG2_E), w[:, aq_hi:],
         w[:, glr_lo:glr_hi], jnp.zeros((D_MODEL, GLR_COLS - GLA_GATE_RANK), w.dtype)],
        axis=1).astype(BF16)


def kernel(x, p, w_in, gla_w_lr_up, gla_b_lr_up, gla_norm_g, rel_bias, lru_conv_w, lru_conv_b,
           lru_w_a, lru_b_a, lru_w_i, lru_b_i, lru_lambda, w_out, ln1_g, ln1_b, w_ffn_gate,
           w_ffn_up, w_ffn_down, w_ple_gate, w_ple_proj, ln2_g, ln2_b):
    batch, seq, _ = x.shape
    n = batch * seq
    assert seq % ROW_TILE == 0
    x2 = x.reshape(n, D_MODEL)
    bias_big = _attention_bias(rel_bias)
    row = lambda a: a.reshape(1, -1).astype(F32)

    for i in range(DEPTH):
        wup_pad = jnp.concatenate(
            [gla_w_lr_up[i], jnp.zeros((GLR_COLS - GLA_GATE_RANK, GLA_WIDTH), F32)],
            axis=0).astype(BF16)

        gla_h, att_h, lru_h, glr = _in_proj(x2, _permute_w_in(w_in[i]))
        y_gla = _gla(gla_h, glr, wup_pad, row(gla_b_lr_up[i]),
                     row(jnp.tile(gla_norm_g[i], GLA_HEADS)), batch)
        y_att = _attention(att_h, bias_big, batch)
        y_lru = _lru(lru_h, lru_conv_w[i].astype(F32), row(lru_conv_b[i]),
                     _block_diag(lru_w_a[i]).astype(BF16), row(lru_b_a[i]),
                     _block_diag(lru_w_i[i]).astype(BF16), row(lru_b_i[i]),
                     row(lru_lambda[i]), batch)
        x2 = _post(y_gla, y_att, y_lru, x2, p[i].reshape(n, PLE_DIM),
                   w_out[i].astype(BF16), row(ln1_g[i]), row(ln1_b[i]),
                   w_ffn_gate[i].astype(BF16), w_ffn_up[i].astype(BF16),
                   w_ffn_down[i].astype(BF16), w_ple_gate[i].astype(BF16),
                   w_ple_proj[i].astype(BF16), row(ln2_g[i]), row(ln2_b[i]))
    return x2.reshape(batch, seq, D_MODEL)
```

```python
import math

import numpy as np
import jax
import jax.numpy as jnp
from jax import lax
from jax.experimental import pallas as pl
from jax.experimental.pallas import tpu as pltpu

F32 = jnp.float32
BF16 = jnp.bfloat16

D_MODEL = 1024
CHUNK = 64
GLA_HEADS = 4
GLA_DK = 64
GLA_WIDTH = 256
GLA_GATE_RANK = 16
GLA_GATE_TAU = 16.0
ATT_HEADS = 8
ATT_DH = 64
ATT_WIDTH = 512
ATT_LEFT_CHUNKS = 8
REL_CLIP = 256
LRU_WIDTH = 256
LRU_BLOCKS = 4
LRU_C = 8.0
CONV_WIDTH = 4
D_FF = 2816
PLE_DIM = 256
DEPTH = 2
DN_ALPHA = (2 * DEPTH) ** 0.25
LN_EPS = 1e-5
RMS_EPS = 1e-6
NEG_INF = -1e30
LOG2_E = math.log2(math.e)

LANES = 128
SUBLANES = 8
VMEM_LIMIT_BYTES = 56 * 1024 * 1024

GLA_COLS = 4 * GLA_WIDTH
ATT_COLS = 3 * ATT_WIDTH
LRU_COLS = 2 * LRU_WIDTH
GLR_COLS = LANES
ATT_OFF = GLA_COLS
LRU_OFF = ATT_OFF + ATT_COLS
GLR_OFF = LRU_OFF + LRU_COLS
D_IN_PAD = GLR_OFF + GLR_COLS

ROW_TILE = 512
ATT_SUB = 256
ATT_WIN = ATT_SUB + ATT_LEFT_CHUNKS * CHUNK
FF_TILE = 256
N_GROUPS = ROW_TILE // SUBLANES


def _dot(a, b):
    return jnp.dot(a, b, preferred_element_type=F32)


def _dot_nt(a, b):
    return lax.dot_general(a, b, (((1,), (1,)), ((), ())), preferred_element_type=F32)


def _dot_tn(a, b):
    return lax.dot_general(a, b, (((0,), (0,)), ((), ())), preferred_element_type=F32)


def _block_diag_mask(n, blk):
    r = lax.broadcasted_iota(jnp.int32, (n, n), 0) // blk
    c = lax.broadcasted_iota(jnp.int32, (n, n), 1) // blk
    return r == c


def _layer_norm(z, g, b):
    mu = jnp.mean(z, axis=-1, keepdims=True)
    zc = z - mu
    var = jnp.mean(zc * zc, axis=-1, keepdims=True)
    return zc * lax.rsqrt(var + LN_EPS) * g + b


def _softplus(x):
    return jnp.maximum(x, 0.0) + jnp.log1p(jnp.exp(-jnp.abs(x)))


def _shift_rows(x, d, fill, row_idx):
    return jnp.where(row_idx >= d, pltpu.roll(x, d, axis=0), fill)


def _gla_tile(h, lr, wup_ref, bup_ref, ng_ref, bdm_ref, o_ref, state_ref):
    t = ROW_TILE
    z = _dot(lr.astype(BF16), wup_ref[...]) + bup_ref[...]
    la = -_softplus(-z) * (1.0 / GLA_GATE_TAU)
    la_hi = la.astype(BF16)
    la_lo = (la - la_hi.astype(F32)).astype(BF16)
    ri = lax.broadcasted_iota(jnp.int32, (CHUNK, CHUNK), 0)
    ci = lax.broadcasted_iota(jnp.int32, (CHUNK, CHUNK), 1)
    tri = jnp.where(ci <= ri, 1.0, 0.0).astype(BF16)
    bcum, refb, lastb = [], [], []
    for c in range(t // CHUNK):
        rows = slice(c * CHUNK, (c + 1) * CHUNK)
        bc = _dot(tri, la_hi[rows]) + _dot(tri, la_lo[rows])
        bcum.append(bc)
        refb.append(jnp.broadcast_to(bc[CHUNK // 2:CHUNK // 2 + 1], bc.shape))
        lastb.append(jnp.broadcast_to(bc[CHUNK - 1:CHUNK], bc.shape))
    bcum = jnp.concatenate(bcum, axis=0)
    refb = jnp.concatenate(refb, axis=0)
    lastb = jnp.concatenate(lastb, axis=0)

    q = h[:, 0:GLA_WIDTH] * (GLA_DK ** -0.5)
    k = h[:, GLA_WIDTH:2 * GLA_WIDTH]
    qe = (q * jnp.exp(bcum - refb)).astype(BF16)
    ke = (k * jnp.exp(refb - bcum)).astype(BF16)
    qd = (q * jnp.exp(bcum)).astype(BF16)
    kd = (k * jnp.exp(lastb - bcum)).astype(BF16)
    vb = h[:, 2 * GLA_WIDTH:3 * GLA_WIDTH].astype(BF16)

    bd = _block_diag_mask(GLA_WIDTH, GLA_DK)
    bd16 = bdm_ref[...] > 0
    zero16 = jnp.zeros((GLA_WIDTH, GLA_WIDTH), BF16)
    ri4 = lax.broadcasted_iota(jnp.int32, (CHUNK, GLA_WIDTH), 0)
    ci4 = lax.broadcasted_iota(jnp.int32, (CHUNK, GLA_WIDTH), 1) % CHUNK
    causal4 = ci4 <= ri4

    state = state_ref[...]
    for c in range(t // CHUNK):
        rows = slice(c * CHUNK, (c + 1) * CHUNK)
        kstack = jnp.where(bd16, jnp.concatenate([ke[rows]] * GLA_HEADS, axis=0), zero16)
        vstack = jnp.where(bd16, jnp.concatenate([vb[rows]] * GLA_HEADS, axis=0), zero16)
        attn = _dot_nt(qe[rows], kstack)
        attn = jnp.where(causal4, attn, 0.0).astype(BF16)
        o_ref[rows, :] = _dot(attn, vstack) + _dot_nt(qd[rows], state.astype(BF16))
        ut = _dot_tn(vb[rows], kd[rows])
        decay = jnp.exp(lastb[c * CHUNK:c * CHUNK + 1, :])
        state = state * decay + jnp.where(bd, ut, 0.0)
    state_ref[...] = state

    o = o_ref[...]
    group_mean = jnp.where(bd16, jnp.full_like(zero16, 1.0 / GLA_DK), zero16)
    ms = _dot((o * o).astype(BF16), group_mean)
    g = h[:, 3 * GLA_WIDTH:4 * GLA_WIDTH]
    return o * lax.rsqrt(ms + RMS_EPS) * ng_ref[...] * (g * jax.nn.sigmoid(g))


def _lru_tile(h, cw_ref, cb_ref, wa_ref, ba_ref, wi_ref, bi_ref, lam_ref,
              xbuf, hcarry, a_scr, u_scr, h_scr, hin_scr):
    t = ROW_TILE
    xbuf[SUBLANES:SUBLANES + t, :] = h[:, 0:LRU_WIDTH]
    xc = cb_ref[...] + jnp.zeros((t, LRU_WIDTH), F32)
    for w in range(CONV_WIDTH):
        off = SUBLANES - (CONV_WIDTH - 1) + w
        xc = xc + cw_ref[w:w + 1, :] * xbuf[off:off + t, :]
    xbuf[0:SUBLANES, :] = xbuf[t:t + SUBLANES, :]

    xcb = xc.astype(BF16)
    r = jax.nn.sigmoid(_dot(xcb, wa_ref[...]) + ba_ref[...])
    ig = jax.nn.sigmoid(_dot(xcb, wi_ref[...]) + bi_ref[...])
    log_a = (-LRU_C) * r * _softplus(-lam_ref[...])
    a = jnp.exp(log_a)
    th = jnp.tanh(log_a)
    u = jnp.sqrt(-2.0 * th / (1.0 - th)) * (ig * xc)

    row_in_group = lax.broadcasted_iota(jnp.int32, (t, LRU_WIDTH), 0) % SUBLANES
    d = 1
    while d < SUBLANES:
        u = u + a * _shift_rows(u, d, 0.0, row_in_group)
        a = a * _shift_rows(a, d, 1.0, row_in_group)
        d *= 2
    lane_halves = [slice(v * LANES, (v + 1) * LANES) for v in range(LRU_WIDTH // LANES)]
    group_last = pl.ds(SUBLANES - 1, N_GROUPS, stride=SUBLANES)
    for v, lanes in enumerate(lane_halves):
        a_scr[v] = a[:, lanes]
        u_scr[v] = u[:, lanes]
    a_g = jnp.concatenate([a_scr[v, group_last, :] for v in range(len(lane_halves))], axis=1)
    u_g = jnp.concatenate([u_scr[v, group_last, :] for v in range(len(lane_halves))], axis=1)
    group_idx = lax.broadcasted_iota(jnp.int32, (N_GROUPS, LRU_WIDTH), 0)
    d = 1
    while d < N_GROUPS:
        u_g = u_g + a_g * _shift_rows(u_g, d, 0.0, group_idx)
        a_g = a_g * _shift_rows(a_g, d, 1.0, group_idx)
        d *= 2
    h_prev = hcarry[0:1, :]
    h_end = u_g + a_g * h_prev
    hin_scr[...] = _shift_rows(h_end, 1, h_prev, group_idx)
    hcarry[...] = jnp.broadcast_to(h_end[N_GROUPS - 1:N_GROUPS, :], hcarry.shape)
    for g in range(N_GROUPS):
        rows = slice(g * SUBLANES, (g + 1) * SUBLANES)
        for v, lanes in enumerate(lane_halves):
            h_scr[rows, lanes] = u_scr[v, rows, :] + a_scr[v, rows, :] * hin_scr[g:g + 1, lanes]
    return h_scr[...] * jax.nn.gelu(h[:, LRU_WIDTH:2 * LRU_WIDTH])


def _mixer_in_kernel(x_ref, w_ref, wup_ref, bup_ref, ng_ref, bdm_ref,
                     cw_ref, cb_ref, wa_ref, ba_ref, wi_ref, bi_ref, lam_ref,
                     att_ref, yg_ref, yl_ref,
                     o_scr, state_scr, xbuf, hcarry, a_scr, u_scr, h_scr, hin_scr):
    @pl.when(pl.program_id(1) == 0)
    def _():
        state_scr[...] = jnp.zeros_like(state_scr)
        xbuf[0:SUBLANES, :] = jnp.zeros((SUBLANES, LRU_WIDTH), F32)
        hcarry[...] = jnp.zeros_like(hcarry)

    xb = x_ref[...].astype(BF16)
    h_lru = _dot(xb, w_ref[:, LRU_OFF:LRU_OFF + LRU_COLS])
    h_glr = _dot(xb, w_ref[:, GLR_OFF:GLR_OFF + GLR_COLS])
    h_gla = _dot(xb, w_ref[:, 0:GLA_COLS])
    half = ATT_COLS // 2
    att_ref[:, 0:half] = _dot(xb, w_ref[:, ATT_OFF:ATT_OFF + half]).astype(BF16)
    yl_ref[...] = _lru_tile(h_lru, cw_ref, cb_ref, wa_ref, ba_ref, wi_ref, bi_ref, lam_ref,
                            xbuf, hcarry, a_scr, u_scr, h_scr, hin_scr).astype(BF16)
    att_ref[:, half:ATT_COLS] = _dot(xb, w_ref[:, ATT_OFF + half:ATT_OFF + ATT_COLS]).astype(BF16)
    yg_ref[...] = _gla_tile(h_gla, h_glr, wup_ref, bup_ref, ng_ref, bdm_ref,
                            o_scr, state_scr).astype(BF16)


def _mixer_in(x2, w_perm, wup_pad, bup, ng_row, conv_w, conv_b, wa_bd, b_a, wi_bd, b_i, lam, batch):
    n = x2.shape[0]
    nt = n // batch // ROW_TILE
    d = np.arange(GLA_WIDTH) // GLA_DK
    bdm = jnp.asarray((d[:, None] == d[None, :]).astype(np.float32), BF16)
    row = lambda c: pl.BlockSpec((ROW_TILE, c), lambda b, i: (b * nt + i, 0))
    full = lambda r, c: pl.BlockSpec((r, c), lambda b, i: (0, 0))
    halves_f32 = pltpu.VMEM((LRU_WIDTH // LANES, ROW_TILE, LANES), F32)
    return pl.pallas_call(
        _mixer_in_kernel,
        grid=(batch, nt),
        in_specs=[row(D_MODEL),
                  pl.BlockSpec((D_MODEL, D_IN_PAD), lambda b, i: (0, 0), pipeline_mode=pl.Buffered(1)),
                  full(GLR_COLS, GLA_WIDTH), full(1, GLA_WIDTH), full(1, GLA_WIDTH),
                  full(GLA_WIDTH, GLA_WIDTH),
                  full(CONV_WIDTH, LRU_WIDTH), full(1, LRU_WIDTH),
                  full(LRU_WIDTH, LRU_WIDTH), full(1, LRU_WIDTH),
                  full(LRU_WIDTH, LRU_WIDTH), full(1, LRU_WIDTH), full(1, LRU_WIDTH)],
        out_specs=[row(ATT_COLS), row(GLA_WIDTH), row(LRU_WIDTH)],
        out_shape=[jax.ShapeDtypeStruct((n, ATT_COLS), BF16),
                   jax.ShapeDtypeStruct((n, GLA_WIDTH), BF16),
                   jax.ShapeDtypeStruct((n, LRU_WIDTH), BF16)],
        scratch_shapes=[pltpu.VMEM((ROW_TILE, GLA_WIDTH), F32),
                        pltpu.VMEM((GLA_WIDTH, GLA_WIDTH), F32),
                        pltpu.VMEM((ROW_TILE + SUBLANES, LRU_WIDTH), F32),
                        pltpu.VMEM((SUBLANES, LRU_WIDTH), F32),
                        halves_f32, halves_f32,
                        pltpu.VMEM((ROW_TILE, LRU_WIDTH), F32),
                        pltpu.VMEM((N_GROUPS, LRU_WIDTH), F32)],
        compiler_params=pltpu.CompilerParams(
            dimension_semantics=("arbitrary", "arbitrary"), vmem_limit_bytes=VMEM_LIMIT_BYTES),
        name="mixer_in",
    )(x2, w_perm, wup_pad, bup, ng_row, bdm, conv_w, conv_b, wa_bd, b_a, wi_bd, b_i, lam)


def _att_kernel(q_ref, kw_ref, vw_ref, bias_ref, hm_ref, y_ref):
    lane_head0 = lax.broadcasted_iota(jnp.int32, (ATT_SUB, LANES), 1) < ATT_DH

    def body(first):
        n_keys = ROW_TILE if first else 2 * ROW_TILE
        head0 = hm_ref[0:n_keys, :] > 0
        v = vw_ref[0:n_keys, :]
        one16 = jnp.ones_like(v)
        vh = (jnp.where(head0, v, one16), jnp.where(head0, one16, v))
        q = q_ref[...]
        zero16 = jnp.zeros_like(q)
        qh = (jnp.where(head0[0:ROW_TILE], q, zero16), jnp.where(head0[0:ROW_TILE], zero16, q))

        def window(j):
            if first:
                return slice(0, (j + 1) * ATT_SUB)
            return slice(j * ATT_SUB, j * ATT_SUB + ATT_WIN)

        def scores(j, hh):
            win = window(j)
            n_win = win.stop - win.start
            rows = slice(j * ATT_SUB, (j + 1) * ATT_SUB)
            return (_dot_nt(qh[hh][rows], kw_ref[win, :])
                    + bias_ref[hh, :, ATT_WIN - n_win:ATT_WIN])

        units = [(j, hh) for j in range(ROW_TILE // ATT_SUB) for hh in range(LANES // ATT_DH)]
        n_units = len(units)
        s, p, outs = {}, {}, []
        for step in range(n_units + 2):
            if step < n_units:
                s[step] = scores(*units[step])
            if step >= 2:
                j, hh = units[step - 2]
                of = _dot(p.pop(step - 2), vh[hh][window(j)])
                outs.append(of / pltpu.roll(of, ATT_DH, axis=1))
                if hh == LANES // ATT_DH - 1:
                    rows = slice(j * ATT_SUB, (j + 1) * ATT_SUB)
                    y_ref[rows, :] = jnp.where(lane_head0, outs[0], outs[1]).astype(BF16)
                    outs = []
            if 1 <= step <= n_units:
                su = s.pop(step - 1)
                m = jnp.max(su, axis=-1, keepdims=True)
                p[step - 1] = jnp.exp2(su - m).astype(BF16)

    @pl.when(pl.program_id(2) == 0)
    def _():
        body(True)

    @pl.when(pl.program_id(2) > 0)
    def _():
        body(False)


def _attention(att_h, bias_big, batch):
    n = att_h.shape[0]
    seq = n // batch
    att3 = att_h.reshape(batch, seq, ATT_COLS)
    pairs = ATT_WIDTH // LANES
    head0 = jnp.asarray(
        np.broadcast_to(np.arange(LANES) < ATT_DH, (2 * ROW_TILE, LANES)).astype(np.float32), BF16)
    window = lambda off: pl.BlockSpec(
        (None, pl.Element(2 * ROW_TILE), pl.Element(LANES)),
        lambda p, b, i: (b, jnp.maximum(i - 1, 0) * ROW_TILE, (off + p) * LANES))
    y = pl.pallas_call(
        _att_kernel,
        grid=(pairs, batch, seq // ROW_TILE),
        in_specs=[pl.BlockSpec((None, ROW_TILE, LANES), lambda p, b, i: (b, i, p)),
                  window(pairs), window(2 * pairs),
                  pl.BlockSpec((LANES // ATT_DH, ATT_SUB, ATT_WIN), lambda p, b, i: (p, 0, 0)),
                  pl.BlockSpec((2 * ROW_TILE, LANES), lambda p, b, i: (0, 0))],
        out_specs=pl.BlockSpec((None, ROW_TILE, LANES), lambda p, b, i: (b, i, p)),
        out_shape=jax.ShapeDtypeStruct((batch, seq, ATT_WIDTH), BF16),
        compiler_params=pltpu.CompilerParams(
            dimension_semantics=("arbitrary", "arbitrary", "arbitrary"),
            vmem_limit_bytes=VMEM_LIMIT_BYTES),
        name="band_attention",
    )(att3, att3, att3, bias_big, head0)
    return y.reshape(n, ATT_WIDTH)


def _attention_bias(rel_bias):
    r = np.arange(ATT_SUB)
    w = np.arange(ATT_WIN)
    kk = w[None, :] - (r[:, None] // CHUNK) * CHUNK
    in_band = (kk >= 0) & (kk < (ATT_LEFT_CHUNKS + 1) * CHUNK)
    n_diag = ATT_SUB + ATT_WIN - 1
    rel = ATT_LEFT_CHUNKS * CHUNK + (ATT_SUB - 1) - np.arange(n_diag)
    idx = np.clip(rel, -REL_CLIP, REL_CLIP) + REL_CLIP
    diag = jnp.pad(rel_bias.astype(F32)[:, idx] * LOG2_E, ((0, 0), (0, 1)))
    skew = jnp.tile(diag, (1, ATT_SUB))[:, :ATT_SUB * n_diag].reshape(-1, ATT_SUB, n_diag)
    table = skew[:, :, ATT_SUB - 1:ATT_SUB - 1 + ATT_WIN]
    return jnp.where(in_band[None], table, NEG_INF)


def _post_kernel(yg_ref, ya_ref, yl_ref, x_ref, p_ref, wo_ref, g1_ref, b1_ref,
                 wg_ref, wu_ref, wd_ref, wpg_ref, wpp_ref, g2_ref, b2_ref, o_ref, act_ref):
    mix = (_dot(yg_ref[...], wo_ref[0:GLA_WIDTH, :])
           + _dot(ya_ref[...], wo_ref[GLA_WIDTH:GLA_WIDTH + ATT_WIDTH, :])
           + _dot(yl_ref[...], wo_ref[GLA_WIDTH + ATT_WIDTH:D_MODEL, :]))
    x1 = _layer_norm(DN_ALPHA * x_ref[...] + mix, g1_ref[...], b1_ref[...])
    xb = x1.astype(BF16)
    for c in range(D_FF // FF_TILE):
        cols = slice(c * FF_TILE, (c + 1) * FF_TILE)
        gate = _dot(xb, wg_ref[:, cols])
        up = _dot(xb, wu_ref[:, cols])
        act_ref[:, cols] = (gate * jax.nn.sigmoid(gate) * up).astype(BF16)
    ffn = _dot(act_ref[...], wd_ref[...])
    ple = jax.nn.sigmoid(_dot(xb, wpg_ref[...])) * _dot(p_ref[...].astype(BF16), wpp_ref[...])
    o_ref[...] = _layer_norm(DN_ALPHA * x1 + ffn + ple, g2_ref[...], b2_ref[...])


def _post(y_gla, y_att, y_lru, x2, p2, wo, g1, b1, wg, wu, wd, wpg, wpp, g2, b2):
    n = x2.shape[0]
    row = lambda c: pl.BlockSpec((ROW_TILE, c), lambda i: (i, 0))
    full = lambda r, c: pl.BlockSpec((r, c), lambda i: (0, 0), pipeline_mode=pl.Buffered(1))
    return pl.pallas_call(
        _post_kernel,
        grid=(n // ROW_TILE,),
        in_specs=[row(GLA_WIDTH), row(ATT_WIDTH), row(LRU_WIDTH), row(D_MODEL), row(PLE_DIM),
                  full(D_MODEL, D_MODEL), full(1, D_MODEL), full(1, D_MODEL),
                  full(D_MODEL, D_FF), full(D_MODEL, D_FF), full(D_FF, D_MODEL),
                  full(D_MODEL, D_MODEL), full(PLE_DIM, D_MODEL),
                  full(1, D_MODEL), full(1, D_MODEL)],
        out_specs=row(D_MODEL),
        out_shape=jax.ShapeDtypeStruct((n, D_MODEL), F32),
        scratch_shapes=[pltpu.VMEM((ROW_TILE, D_FF), BF16)],
        compiler_params=pltpu.CompilerParams(
            dimension_semantics=("arbitrary",), vmem_limit_bytes=VMEM_LIMIT_BYTES),
        name="out_proj_ffn",
    )(y_gla, y_att, y_lru, x2, p2, wo, g1, b1, wg, wu, wd, wpg, wpp, g2, b2)


def _block_diag(w):
    g, c, d = w.shape
    eye = jnp.eye(g, dtype=w.dtype)
    return (eye[:, None, :, None] * w[:, :, None, :]).reshape(g * c, g * d)


def _permute_w_in(w):
    glr_lo = GLA_COLS
    glr_hi = GLA_COLS + GLA_GATE_RANK
    aq_hi = glr_hi + ATT_WIDTH
    return jnp.concatenate(
        [w[:, :glr_lo], w[:, glr_hi:aq_hi] * (ATT_DH ** -0.5 * LOG2_E), w[:, aq_hi:],
         w[:, glr_lo:glr_hi], jnp.zeros((D_MODEL, GLR_COLS - GLA_GATE_RANK), w.dtype)],
        axis=1).astype(BF16)


def kernel(x, p, w_in, gla_w_lr_up, gla_b_lr_up, gla_norm_g, rel_bias, lru_conv_w, lru_conv_b,
           lru_w_a, lru_b_a, lru_w_i, lru_b_i, lru_lambda, w_out, ln1_g, ln1_b, w_ffn_gate,
           w_ffn_up, w_ffn_down, w_ple_gate, w_ple_proj, ln2_g, ln2_b):
    batch, seq, _ = x.shape
    n = batch * seq
    assert seq % ROW_TILE == 0
    x2 = x.reshape(n, D_MODEL)
    bias_big = _attention_bias(rel_bias)
    row = lambda a: a.reshape(1, -1).astype(F32)

    for i in range(DEPTH):
        wup_pad = jnp.concatenate(
            [gla_w_lr_up[i], jnp.zeros((GLR_COLS - GLA_GATE_RANK, GLA_WIDTH), F32)],
            axis=0).astype(BF16)
        att_h, y_gla, y_lru = _mixer_in(
            x2, _permute_w_in(w_in[i]), wup_pad, row(gla_b_lr_up[i]),
            row(jnp.tile(gla_norm_g[i], GLA_HEADS)),
            lru_conv_w[i].astype(F32), row(lru_conv_b[i]),
            _block_diag(lru_w_a[i]).astype(BF16), row(lru_b_a[i]),
            _block_diag(lru_w_i[i]).astype(BF16), row(lru_b_i[i]), row(lru_lambda[i]), batch)
        y_att = _attention(att_h, bias_big, batch)
        x2 = _post(y_gla, y_att, y_lru, x2, p[i].reshape(n, PLE_DIM),
                   w_out[i].astype(BF16), row(ln1_g[i]), row(ln1_b[i]),
                   w_ffn_gate[i].astype(BF16), w_ffn_up[i].astype(BF16),
                   w_ffn_down[i].astype(BF16), w_ple_gate[i].astype(BF16),
                   w_ple_proj[i].astype(BF16), row(ln2_g[i]), row(ln2_b[i]))
    return x2.reshape(batch, seq, D_MODEL)
```

```python
import itertools
import math

import numpy as np
import jax
import jax.numpy as jnp
from jax import lax
from jax.experimental import pallas as pl
from jax.experimental.pallas import tpu as pltpu

F32 = jnp.float32
BF16 = jnp.bfloat16

D_MODEL = 1024
CHUNK = 64
GLA_HEADS = 4
GLA_DK = 64
GLA_WIDTH = 256
GLA_GATE_RANK = 16
GLA_GATE_TAU = 16.0
ATT_HEADS = 8
ATT_DH = 64
ATT_WIDTH = 512
ATT_LEFT_CHUNKS = 8
REL_CLIP = 256
LRU_WIDTH = 256
LRU_BLOCKS = 4
LRU_C = 8.0
CONV_WIDTH = 4
D_FF = 2816
PLE_DIM = 256
DEPTH = 2
DN_ALPHA = (2 * DEPTH) ** 0.25
LN_EPS = 1e-5
RMS_EPS = 1e-6
NEG_INF = -1e30
LOG2_E = math.log2(math.e)

LANES = 128
SUBLANES = 8
VMEM_LIMIT_BYTES = 56 * 1024 * 1024

GLA_COLS = 4 * GLA_WIDTH
ATT_COLS = 3 * ATT_WIDTH
LRU_COLS = 2 * LRU_WIDTH
GLR_COLS = LANES
ATT_OFF = GLA_COLS
LRU_OFF = ATT_OFF + ATT_COLS
GLR_OFF = LRU_OFF + LRU_COLS
D_IN_PAD = GLR_OFF + GLR_COLS

ROW_TILE = 512
ATT_SUB = 256
ATT_WIN = ATT_SUB + ATT_LEFT_CHUNKS * CHUNK
ATT_DIAG = ATT_SUB + ATT_WIN
FF_TILE = 256
PROJ_PIECE = 256
SUB_ROWS = 512
POST_TILE = 512
POST_ROWS = 256


def _dot(a, b):
    return jnp.dot(a, b, preferred_element_type=F32)


def _dot_nt(a, b):
    return lax.dot_general(a, b, (((1,), (1,)), ((), ())), preferred_element_type=F32)


def _dot_tn(a, b):
    return lax.dot_general(a, b, (((0,), (0,)), ((), ())), preferred_element_type=F32)


def _block_diag_mask(n, blk):
    r = lax.broadcasted_iota(jnp.int32, (n, n), 0) // blk
    c = lax.broadcasted_iota(jnp.int32, (n, n), 1) // blk
    return r == c


def _layer_norm(z, g, b):
    mu = jnp.mean(z, axis=-1, keepdims=True)
    zc = z - mu
    var = jnp.mean(zc * zc, axis=-1, keepdims=True)
    return zc * lax.rsqrt(var + LN_EPS) * g + b


def _softplus(x):
    return jnp.maximum(x, 0.0) + jnp.log1p(jnp.exp(-jnp.abs(x)))


def _shift_rows(x, d, fill, row_idx):
    return jnp.where(row_idx >= d, pltpu.roll(x, d, axis=0), fill)


def _gla_rows(get_h, wup_ref, bup_ref, ng_ref, bdm_ref, o_ref, state_ref, y_ref):
    h, lr = get_h()
    t = h.shape[0]
    z = _dot(lr.astype(BF16), wup_ref[...]) + bup_ref[...]
    la = -_softplus(-z) * (1.0 / GLA_GATE_TAU)
    la_hi = la.astype(BF16)
    la_lo = (la - la_hi.astype(F32)).astype(BF16)
    ri = lax.broadcasted_iota(jnp.int32, (CHUNK, CHUNK), 0)
    ci = lax.broadcasted_iota(jnp.int32, (CHUNK, CHUNK), 1)
    tri = jnp.where(ci <= ri, 1.0, 0.0).astype(BF16)
    bcum, refb, lastb = [], [], []
    for c in range(t // CHUNK):
        rows = slice(c * CHUNK, (c + 1) * CHUNK)
        bc = _dot(tri, la_hi[rows]) + _dot(tri, la_lo[rows])
        bcum.append(bc)
        refb.append(jnp.broadcast_to(bc[CHUNK // 2:CHUNK // 2 + 1], bc.shape))
        lastb.append(jnp.broadcast_to(bc[CHUNK - 1:CHUNK], bc.shape))
    bcum = jnp.concatenate(bcum, axis=0)
    refb = jnp.concatenate(refb, axis=0)
    lastb = jnp.concatenate(lastb, axis=0)
    yield

    q = h[:, 0:GLA_WIDTH] * (GLA_DK ** -0.5)
    k = h[:, GLA_WIDTH:2 * GLA_WIDTH]
    qe = (q * jnp.exp(bcum - refb)).astype(BF16)
    ke = (k * jnp.exp(refb - bcum)).astype(BF16)
    qd = (q * jnp.exp(bcum)).astype(BF16)
    kd = (k * jnp.exp(lastb - bcum)).astype(BF16)
    vb = h[:, 2 * GLA_WIDTH:3 * GLA_WIDTH].astype(BF16)
    yield

    bd = _block_diag_mask(GLA_WIDTH, GLA_DK)
    bd16 = bdm_ref[...] > 0
    zero16 = jnp.zeros((GLA_WIDTH, GLA_WIDTH), BF16)
    ri4 = lax.broadcasted_iota(jnp.int32, (CHUNK, GLA_WIDTH), 0)
    ci4 = lax.broadcasted_iota(jnp.int32, (CHUNK, GLA_WIDTH), 1) % CHUNK
    causal4 = ci4 <= ri4

    state = state_ref[...]
    for c in range(t // CHUNK):
        rows = slice(c * CHUNK, (c + 1) * CHUNK)
        kstack = jnp.where(bd16, jnp.concatenate([ke[rows]] * GLA_HEADS, axis=0), zero16)
        vstack = jnp.where(bd16, jnp.concatenate([vb[rows]] * GLA_HEADS, axis=0), zero16)
        attn = _dot_nt(qe[rows], kstack)
        attn = jnp.where(causal4, attn, 0.0).astype(BF16)
        o_ref[rows, :] = _dot(attn, vstack) + _dot_nt(qd[rows], state.astype(BF16))
        ut = _dot_tn(vb[rows], kd[rows])
        decay = jnp.exp(lastb[c * CHUNK:c * CHUNK + 1, :])
        state = state * decay + jnp.where(bd, ut, 0.0)
        if c % 2 == 1:
            yield
    state_ref[...] = state

    o = o_ref[...]
    group_mean = jnp.where(bd16, jnp.full_like(zero16, 1.0 / GLA_DK), zero16)
    ms = _dot((o * o).astype(BF16), group_mean)
    g = h[:, 3 * GLA_WIDTH:4 * GLA_WIDTH]
    y_ref[...] = (o * lax.rsqrt(ms + RMS_EPS) * ng_ref[...] * (g * jax.nn.sigmoid(g))).astype(BF16)


def _lru_rows(get_h, cw_ref, cb_ref, wa_ref, ba_ref, wi_ref, bi_ref, lam_ref,
              xtail, hcarry, xbuf, a_scr, u_scr, h_scr, hin_scr, y_ref):
    h = get_h()
    t = h.shape[0]
    n_groups = t // SUBLANES
    xbuf[0:SUBLANES, :] = xtail[...]
    xbuf[SUBLANES:SUBLANES + t, :] = h[:, 0:LRU_WIDTH]
    xc = cb_ref[...] + jnp.zeros((t, LRU_WIDTH), F32)
    for w in range(CONV_WIDTH):
        off = SUBLANES - (CONV_WIDTH - 1) + w
        xc = xc + cw_ref[w:w + 1, :] * xbuf[off:off + t, :]
    xtail[...] = xbuf[t:t + SUBLANES, :]
    yield

    xcb = xc.astype(BF16)
    r = jax.nn.sigmoid(_dot(xcb, wa_ref[...]) + ba_ref[...])
    ig = jax.nn.sigmoid(_dot(xcb, wi_ref[...]) + bi_ref[...])
    log_a = (-LRU_C) * r * _softplus(-lam_ref[...])
    a = jnp.exp(log_a)
    th = jnp.tanh(log_a)
    u = jnp.sqrt(-2.0 * th / (1.0 - th)) * (ig * xc)
    yield

    row_in_group = lax.broadcasted_iota(jnp.int32, (t, LRU_WIDTH), 0) % SUBLANES
    d = 1
    while d < SUBLANES:
        u = u + a * _shift_rows(u, d, 0.0, row_in_group)
        a = a * _shift_rows(a, d, 1.0, row_in_group)
        d *= 2
    lane_halves = [slice(v * LANES, (v + 1) * LANES) for v in range(LRU_WIDTH // LANES)]
    group_last = pl.ds(SUBLANES - 1, n_groups, stride=SUBLANES)
    for v, lanes in enumerate(lane_halves):
        a_scr[v] = a[:, lanes]
        u_scr[v] = u[:, lanes]
    yield
    a_g = jnp.concatenate([a_scr[v, group_last, :] for v in range(len(lane_halves))], axis=1)
    u_g = jnp.concatenate([u_scr[v, group_last, :] for v in range(len(lane_halves))], axis=1)
    group_idx = lax.broadcasted_iota(jnp.int32, (n_groups, LRU_WIDTH), 0)
    d = 1
    while d < n_groups:
        u_g = u_g + a_g * _shift_rows(u_g, d, 0.0, group_idx)
        a_g = a_g * _shift_rows(a_g, d, 1.0, group_idx)
        d *= 2
    h_prev = hcarry[0:1, :]
    h_end = u_g + a_g * h_prev
    hin_scr[...] = _shift_rows(h_end, 1, h_prev, group_idx)
    hcarry[...] = jnp.broadcast_to(h_end[n_groups - 1:n_groups, :], hcarry.shape)
    yield
    for g in range(n_groups):
        rows = slice(g * SUBLANES, (g + 1) * SUBLANES)
        for v, lanes in enumerate(lane_halves):
            h_scr[rows, lanes] = u_scr[v, rows, :] + a_scr[v, rows, :] * hin_scr[g:g + 1, lanes]
    y_ref[...] = (h_scr[...] * jax.nn.gelu(h[:, LRU_WIDTH:2 * LRU_WIDTH])).astype(BF16)


def _round_robin(*gens):
    pending = list(gens)
    while pending:
        alive = []
        for g in pending:
            if next(g, StopIteration) is not StopIteration:
                alive.append(g)
                yield
        pending = alive


def _mixer_in_kernel(x_ref, w_ref, wup_ref, bup_ref, ng_ref, bdm_ref,
                     cw_ref, cb_ref, wa_ref, ba_ref, wi_ref, bi_ref, lam_ref,
                     att_ref, yg_ref, yl_ref,
                     state_scr, xtail, hcarry, o_scr, xbuf, a_scr, u_scr, h_scr, hin_scr):
    @pl.when(pl.program_id(1) == 0)
    def _():
        state_scr[...] = jnp.zeros_like(state_scr)
        xtail[...] = jnp.zeros_like(xtail)
        hcarry[...] = jnp.zeros_like(hcarry)

    n_parts = ROW_TILE // SUB_ROWS
    projected = [dict() for _ in range(n_parts)]

    def projection(k):
        rows = slice(k * SUB_ROWS, (k + 1) * SUB_ROWS)
        xb = x_ref[rows, :].astype(BF16)
        out = projected[k]
        for name, off, width in (("lru", LRU_OFF, LRU_COLS), ("glr", GLR_OFF, GLR_COLS),
                                 ("gla", 0, GLA_COLS)):
            cols = []
            for c in range(0, width, PROJ_PIECE):
                n_cols = min(PROJ_PIECE, width - c)
                cols.append(_dot(xb, w_ref[:, off + c:off + c + n_cols]))
                yield
            out[name] = jnp.concatenate(cols, axis=1) if len(cols) > 1 else cols[0]
        for c in range(0, ATT_COLS, PROJ_PIECE):
            att_ref[rows, c:c + PROJ_PIECE] = _dot(
                xb, w_ref[:, ATT_OFF + c:ATT_OFF + c + PROJ_PIECE]).astype(BF16)
            yield

    def mixers(k):
        rows = slice(k * SUB_ROWS, (k + 1) * SUB_ROWS)
        out = projected[k]
        lru = _lru_rows(lambda: out["lru"], cw_ref, cb_ref, wa_ref, ba_ref, wi_ref, bi_ref, lam_ref,
                        xtail, hcarry, xbuf.at[k], a_scr.at[k], u_scr.at[k], h_scr.at[k],
                        hin_scr.at[k], yl_ref.at[rows])
        gla = _gla_rows(lambda: (out["gla"], out["glr"]), wup_ref, bup_ref, ng_ref, bdm_ref,
                        o_scr.at[k], state_scr, yg_ref.at[rows])
        return itertools.chain(lru, gla)

    for _ in projection(0):
        pass
    for k in range(n_parts):
        nxt = projection(k + 1) if k + 1 < n_parts else iter(())
        for _ in _round_robin(nxt, mixers(k)):
            pass


def _mixer_in(x2, w_perm, wup_pad, bup, ng_row, conv_w, conv_b, wa_bd, b_a, wi_bd, b_i, lam, batch):
    n = x2.shape[0]
    nt = n // batch // ROW_TILE
    n_parts = ROW_TILE // SUB_ROWS
    d = np.arange(GLA_WIDTH) // GLA_DK
    bdm = jnp.asarray((d[:, None] == d[None, :]).astype(np.float32), BF16)
    row = lambda c: pl.BlockSpec((ROW_TILE, c), lambda b, i: (b * nt + i, 0))
    full = lambda r, c: pl.BlockSpec((r, c), lambda b, i: (0, 0))
    part = lambda r, c: pltpu.VMEM((n_parts, r, c), F32)
    lane_halves = pltpu.VMEM((n_parts, LRU_WIDTH // LANES, SUB_ROWS, LANES), F32)
    return pl.pallas_call(
        _mixer_in_kernel,
        grid=(batch, nt),
        in_specs=[row(D_MODEL),
                  pl.BlockSpec((D_MODEL, D_IN_PAD), lambda b, i: (0, 0), pipeline_mode=pl.Buffered(1)),
                  full(GLR_COLS, GLA_WIDTH), full(1, GLA_WIDTH), full(1, GLA_WIDTH),
                  full(GLA_WIDTH, GLA_WIDTH),
                  full(CONV_WIDTH, LRU_WIDTH), full(1, LRU_WIDTH),
                  full(LRU_WIDTH, LRU_WIDTH), full(1, LRU_WIDTH),
                  full(LRU_WIDTH, LRU_WIDTH), full(1, LRU_WIDTH), full(1, LRU_WIDTH)],
        out_specs=[row(ATT_COLS), row(GLA_WIDTH), row(LRU_WIDTH)],
        out_shape=[jax.ShapeDtypeStruct((n, ATT_COLS), BF16),
                   jax.ShapeDtypeStruct((n, GLA_WIDTH), BF16),
                   jax.ShapeDtypeStruct((n, LRU_WIDTH), BF16)],
        scratch_shapes=[pltpu.VMEM((GLA_WIDTH, GLA_WIDTH), F32),
                        pltpu.VMEM((SUBLANES, LRU_WIDTH), F32),
                        pltpu.VMEM((SUBLANES, LRU_WIDTH), F32),
                        part(SUB_ROWS, GLA_WIDTH),
                        part(SUB_ROWS + SUBLANES, LRU_WIDTH),
                        lane_halves, lane_halves,
                        part(SUB_ROWS, LRU_WIDTH),
                        part(SUB_ROWS // SUBLANES, LRU_WIDTH)],
        compiler_params=pltpu.CompilerParams(
            dimension_semantics=("arbitrary", "arbitrary"), vmem_limit_bytes=VMEM_LIMIT_BYTES),
        name="mixer_in",
    )(x2, w_perm, wup_pad, bup, ng_row, bdm, conv_w, conv_b, wa_bd, b_a, wi_bd, b_i, lam)


def _att_kernel(q_ref, kw_ref, vw_ref, diag_ref, hm_ref, y_ref, bias_ref):
    lane_head0 = lax.broadcasted_iota(jnp.int32, (ATT_SUB, LANES), 1) < ATT_DH

    @pl.when((pl.program_id(1) == 0) & (pl.program_id(2) == 0))
    def _():
        r = lax.broadcasted_iota(jnp.int32, (ATT_SUB, ATT_WIN), 0)
        w = lax.broadcasted_iota(jnp.int32, (ATT_SUB, ATT_WIN), 1)
        kk = w - (r // CHUNK) * CHUNK
        in_band = (kk >= 0) & (kk < (ATT_LEFT_CHUNKS + 1) * CHUNK)
        for hh in range(LANES // ATT_DH):
            rows = jnp.broadcast_to(diag_ref[hh:hh + 1, :], (ATT_SUB, ATT_DIAG))
            skew = pltpu.roll(rows, ATT_DIAG - (ATT_SUB - 1), axis=1, stride=1, stride_axis=0)
            bias_ref[hh] = jnp.where(in_band, skew[:, 0:ATT_WIN], NEG_INF)

    def body(first):
        n_keys = ROW_TILE if first else 2 * ROW_TILE
        head0 = hm_ref[0:n_keys, :] > 0
        v = vw_ref[0:n_keys, :]
        one16 = jnp.ones_like(v)
        vh = (jnp.where(head0, v, one16), jnp.where(head0, one16, v))
        q = q_ref[...]
        zero16 = jnp.zeros_like(q)
        qh = (jnp.where(head0[0:ROW_TILE], q, zero16), jnp.where(head0[0:ROW_TILE], zero16, q))

        def window(j):
            if first:
                return slice(0, (j + 1) * ATT_SUB)
            return slice(j * ATT_SUB, j * ATT_SUB + ATT_WIN)

        def scores(j, hh):
            win = window(j)
            n_win = win.stop - win.start
            rows = slice(j * ATT_SUB, (j + 1) * ATT_SUB)
            return (_dot_nt(qh[hh][rows], kw_ref[win, :])
                    + bias_ref[hh, :, ATT_WIN - n_win:ATT_WIN])

        units = [(j, hh) for j in range(ROW_TILE // ATT_SUB) for hh in range(LANES // ATT_DH)]
        n_units = len(units)
        s, p, outs = {}, {}, []
        for step in range(n_units + 2):
            if step < n_units:
                s[step] = scores(*units[step])
            if step >= 2:
                j, hh = units[step - 2]
                of = _dot(p.pop(step - 2), vh[hh][window(j)])
                outs.append(of / pltpu.roll(of, ATT_DH, axis=1))
                if hh == LANES // ATT_DH - 1:
                    rows = slice(j * ATT_SUB, (j + 1) * ATT_SUB)
                    y_ref[rows, :] = jnp.where(lane_head0, outs[0], outs[1]).astype(BF16)
                    outs = []
            if 1 <= step <= n_units:
                su = s.pop(step - 1)
                m = jnp.max(su, axis=-1, keepdims=True)
                p[step - 1] = jnp.exp2(su - m).astype(BF16)

    @pl.when(pl.program_id(2) == 0)
    def _():
        body(True)

    @pl.when(pl.program_id(2) > 0)
    def _():
        body(False)


def _attention(att_h, diag, batch):
    n = att_h.shape[0]
    seq = n // batch
    att3 = att_h.reshape(batch, seq, ATT_COLS)
    pairs = ATT_WIDTH // LANES
    head0 = jnp.asarray(
        np.broadcast_to(np.arange(LANES) < ATT_DH, (2 * ROW_TILE, LANES)).astype(np.float32), BF16)
    window = lambda off: pl.BlockSpec(
        (None, pl.Element(2 * ROW_TILE), pl.Element(LANES)),
        lambda p, b, i: (b, jnp.maximum(i - 1, 0) * ROW_TILE, (off + p) * LANES))
    y = pl.pallas_call(
        _att_kernel,
        grid=(pairs, batch, seq // ROW_TILE),
        in_specs=[pl.BlockSpec((None, ROW_TILE, LANES), lambda p, b, i: (b, i, p)),
                  window(pairs), window(2 * pairs),
                  pl.BlockSpec((None, LANES // ATT_DH, ATT_DIAG), lambda p, b, i: (p, 0, 0)),
                  pl.BlockSpec((2 * ROW_TILE, LANES), lambda p, b, i: (0, 0))],
        out_specs=pl.BlockSpec((None, ROW_TILE, LANES), lambda p, b, i: (b, i, p)),
        out_shape=jax.ShapeDtypeStruct((batch, seq, ATT_WIDTH), BF16),
        scratch_shapes=[pltpu.VMEM((LANES // ATT_DH, ATT_SUB, ATT_WIN), F32)],
        compiler_params=pltpu.CompilerParams(
            dimension_semantics=("arbitrary", "arbitrary", "arbitrary"),
            vmem_limit_bytes=VMEM_LIMIT_BYTES),
        name="band_attention",
    )(att3, att3, att3, diag, head0)
    return y.reshape(n, ATT_WIDTH)


def _attention_diag(rel_bias):
    n_diag = ATT_SUB + ATT_WIN - 1
    rel = ATT_LEFT_CHUNKS * CHUNK + (ATT_SUB - 1) - np.arange(n_diag)
    idx = np.clip(rel, -REL_CLIP, REL_CLIP) + REL_CLIP
    diag = jnp.pad(rel_bias.astype(F32)[:, idx] * LOG2_E, ((0, 0), (0, ATT_DIAG - n_diag)))
    return diag.reshape(ATT_WIDTH // LANES, LANES // ATT_DH, ATT_DIAG)


def _post_kernel(yg_ref, ya_ref, yl_ref, x_ref, p_ref, wo_ref, g1_ref, b1_ref,
                 wg_ref, wu_ref, wd_ref, wpg_ref, wpp_ref, g2_ref, b2_ref, o_ref, act_ref):
    n_parts = POST_TILE // POST_ROWS
    rows = [slice(k * POST_ROWS, (k + 1) * POST_ROWS) for k in range(n_parts)]
    n_ff = D_FF // FF_TILE
    mix, x1, xb, ffn, ple = ({} for _ in range(5))

    def out_projection(k):
        mix[k] = (_dot(yg_ref[rows[k], :], wo_ref[0:GLA_WIDTH, :])
                  + _dot(ya_ref[rows[k], :], wo_ref[GLA_WIDTH:GLA_WIDTH + ATT_WIDTH, :])
                  + _dot(yl_ref[rows[k], :], wo_ref[GLA_WIDTH + ATT_WIDTH:D_MODEL, :]))

    def norm1(k):
        x1[k] = _layer_norm(DN_ALPHA * x_ref[rows[k], :] + mix.pop(k), g1_ref[...], b1_ref[...])
        xb[k] = x1[k].astype(BF16)

    def ffn_up(k, c):
        cols = slice(c * FF_TILE, (c + 1) * FF_TILE)
        gate = _dot(xb[k], wg_ref[:, cols])
        up = _dot(xb[k], wu_ref[:, cols])
        act_ref[k, :, cols] = (gate * jax.nn.sigmoid(gate) * up).astype(BF16)

    def ffn_down(k):
        ffn[k] = _dot(act_ref[k], wd_ref[...])
        ple[k] = (jax.nn.sigmoid(_dot(xb.pop(k), wpg_ref[...]))
                  * _dot(p_ref[rows[k], :].astype(BF16), wpp_ref[...]))

    def norm2(k):
        o_ref[rows[k], :] = _layer_norm(DN_ALPHA * x1.pop(k) + ffn.pop(k) + ple.pop(k),
                                        g2_ref[...], b2_ref[...])

    for k in range(n_parts):
        out_projection(k)
    norm1(0)
    for k in range(n_parts):
        for c in range(n_ff):
            ffn_up(k, c)
            if c == n_ff // 2 and k + 1 < n_parts:
                norm1(k + 1)
    for k in range(n_parts):
        ffn_down(k)
        if k > 0:
            norm2(k - 1)
    norm2(n_parts - 1)


def _post(y_gla, y_att, y_lru, x2, p2, wo, g1, b1, wg, wu, wd, wpg, wpp, g2, b2):
    n = x2.shape[0]
    row = lambda c: pl.BlockSpec((POST_TILE, c), lambda i: (i, 0))
    full = lambda r, c: pl.BlockSpec((r, c), lambda i: (0, 0), pipeline_mode=pl.Buffered(1))
    return pl.pallas_call(
        _post_kernel,
        grid=(n // POST_TILE,),
        in_specs=[row(GLA_WIDTH), row(ATT_WIDTH), row(LRU_WIDTH), row(D_MODEL), row(PLE_DIM),
                  full(D_MODEL, D_MODEL), full(1, D_MODEL), full(1, D_MODEL),
                  full(D_MODEL, D_FF), full(D_MODEL, D_FF), full(D_FF, D_MODEL),
                  full(D_MODEL, D_MODEL), full(PLE_DIM, D_MODEL),
                  full(1, D_MODEL), full(1, D_MODEL)],
        out_specs=row(D_MODEL),
        out_shape=jax.ShapeDtypeStruct((n, D_MODEL), F32),
        scratch_shapes=[pltpu.VMEM((POST_TILE // POST_ROWS, POST_ROWS, D_FF), BF16)],
        compiler_params=pltpu.CompilerParams(
            dimension_semantics=("arbitrary",), vmem_limit_bytes=VMEM_LIMIT_BYTES),
        name="out_proj_ffn",
    )(y_gla, y_att, y_lru, x2, p2, wo, g1, b1, wg, wu, wd, wpg, wpp, g2, b2)


def _block_diag(w):
    g, c, d = w.shape
    eye = jnp.eye(g, dtype=w.dtype)
    return (eye[:, None, :, None] * w[:, :, None, :]).reshape(g * c, g * d)


def _permute_w_in(w):
    glr_lo = GLA_COLS
    glr_hi = GLA_COLS + GLA_GATE_RANK
    aq_hi = glr_hi + ATT_WIDTH
    return jnp.concatenate(
        [w[:, :glr_lo], w[:, glr_hi:aq_hi] * (ATT_DH ** -0.5 * LOG2_E), w[:, aq_hi:],
         w[:, glr_lo:glr_hi], jnp.zeros((D_MODEL, GLR_COLS - GLA_GATE_RANK), w.dtype)],
        axis=1).astype(BF16)


def kernel(x, p, w_in, gla_w_lr_up, gla_b_lr_up, gla_norm_g, rel_bias, lru_conv_w, lru_conv_b,
           lru_w_a, lru_b_a, lru_w_i, lru_b_i, lru_lambda, w_out, ln1_g, ln1_b, w_ffn_gate,
           w_ffn_up, w_ffn_down, w_ple_gate, w_ple_proj, ln2_g, ln2_b):
    batch, seq, _ = x.shape
    n = batch * seq
    assert seq % ROW_TILE == 0
    x2 = x.reshape(n, D_MODEL)
    att_diag = _attention_diag(rel_bias)
    row = lambda a: a.reshape(1, -1).astype(F32)

    for i in range(DEPTH):
        wup_pad = jnp.concatenate(
            [gla_w_lr_up[i], jnp.zeros((GLR_COLS - GLA_GATE_RANK, GLA_WIDTH), F32)],
            axis=0).astype(BF16)
        att_h, y_gla, y_lru = _mixer_in(
            x2, _permute_w_in(w_in[i]), wup_pad, row(gla_b_lr_up[i]),
            row(jnp.tile(gla_norm_g[i], GLA_HEADS)),
            lru_conv_w[i].astype(F32), row(lru_conv_b[i]),
            _block_diag(lru_w_a[i]).astype(BF16), row(lru_b_a[i]),
            _block_diag(lru_w_i[i]).astype(BF16), row(lru_b_i[i]), row(lru_lambda[i]), batch)
        y_att = _attention(att_h, att_diag, batch)
        x2 = _post(y_gla, y_att, y_lru, x2, p[i].reshape(n, PLE_DIM),
                   w_out[i].astype(BF16), row(ln1_g[i]), row(ln1_b[i]),
                   w_ffn_gate[i].astype(BF16), w_ffn_up[i].astype(BF16),
                   w_ffn_down[i].astype(BF16), w_ple_gate[i].astype(BF16),
                   w_ple_proj[i].astype(BF16), row(ln2_g[i]), row(ln2_b[i]))
    return x2.reshape(batch, seq, D_MODEL)
```

```python
import itertools
import math

import numpy as np
import jax
import jax.numpy as jnp
from jax import lax
from jax.experimental import pallas as pl
from jax.experimental.pallas import tpu as pltpu

F32 = jnp.float32
BF16 = jnp.bfloat16

D_MODEL = 1024
CHUNK = 64
GLA_HEADS = 4
GLA_DK = 64
GLA_WIDTH = 256
GLA_GATE_RANK = 16
GLA_GATE_TAU = 16.0
ATT_HEADS = 8
ATT_DH = 64
ATT_WIDTH = 512
ATT_LEFT_CHUNKS = 8
REL_CLIP = 256
LRU_WIDTH = 256
LRU_BLOCKS = 4
LRU_C = 8.0
CONV_WIDTH = 4
D_FF = 2816
PLE_DIM = 256
DEPTH = 2
DN_ALPHA = (2 * DEPTH) ** 0.25
LN_EPS = 1e-5
RMS_EPS = 1e-6
NEG_INF = -1e30
LOG2_E = math.log2(math.e)

LANES = 128
SUBLANES = 8
VMEM_LIMIT_BYTES = 56 * 1024 * 1024

GLA_COLS = 4 * GLA_WIDTH
ATT_COLS = 3 * ATT_WIDTH
LRU_COLS = 2 * LRU_WIDTH
GLR_COLS = LANES
ATT_OFF = GLA_COLS
LRU_OFF = ATT_OFF + ATT_COLS
GLR_OFF = LRU_OFF + LRU_COLS
D_IN_PAD = GLR_OFF + GLR_COLS

ROW_TILE = 512
ATT_SUB = 256
ATT_WIN = ATT_SUB + ATT_LEFT_CHUNKS * CHUNK
ATT_DIAG = ATT_SUB + ATT_WIN
FF_TILE = 256
PROJ_PIECE = 256
SUB_ROWS = 512
POST_TILE = 512
POST_ROWS = 256


def _dot(a, b):
    return jnp.dot(a, b, preferred_element_type=F32)


def _dot_nt(a, b):
    return lax.dot_general(a, b, (((1,), (1,)), ((), ())), preferred_element_type=F32)


def _dot_tn(a, b):
    return lax.dot_general(a, b, (((0,), (0,)), ((), ())), preferred_element_type=F32)


def _block_diag_mask(n, blk):
    r = lax.broadcasted_iota(jnp.int32, (n, n), 0) // blk
    c = lax.broadcasted_iota(jnp.int32, (n, n), 1) // blk
    return r == c


def _layer_norm(z, g, b):
    mu = jnp.mean(z, axis=-1, keepdims=True)
    zc = z - mu
    var = jnp.mean(zc * zc, axis=-1, keepdims=True)
    return zc * lax.rsqrt(var + LN_EPS) * g + b


def _softplus(x):
    return jnp.maximum(x, 0.0) + jnp.log1p(jnp.exp(-jnp.abs(x)))


def _shift_rows(x, d, fill, row_idx):
    return jnp.where(row_idx >= d, pltpu.roll(x, d, axis=0), fill)


def _gla_rows(get_h, wup_ref, bup_ref, ng_ref, bdm_ref, o_ref, state_ref, y_ref):
    h, lr = get_h()
    t = h.shape[0]
    z = _dot(lr.astype(BF16), wup_ref[...]) + bup_ref[...]
    la = -_softplus(-z) * (1.0 / GLA_GATE_TAU)
    la_hi = la.astype(BF16)
    la_lo = (la - la_hi.astype(F32)).astype(BF16)
    ri = lax.broadcasted_iota(jnp.int32, (CHUNK, CHUNK), 0)
    ci = lax.broadcasted_iota(jnp.int32, (CHUNK, CHUNK), 1)
    tri = jnp.where(ci <= ri, 1.0, 0.0).astype(BF16)
    bcum, refb, lastb = [], [], []
    for c in range(t // CHUNK):
        rows = slice(c * CHUNK, (c + 1) * CHUNK)
        bc = _dot(tri, la_hi[rows]) + _dot(tri, la_lo[rows])
        bcum.append(bc)
        refb.append(jnp.broadcast_to(bc[CHUNK // 2:CHUNK // 2 + 1], bc.shape))
        lastb.append(jnp.broadcast_to(bc[CHUNK - 1:CHUNK], bc.shape))
    bcum = jnp.concatenate(bcum, axis=0)
    refb = jnp.concatenate(refb, axis=0)
    lastb = jnp.concatenate(lastb, axis=0)
    yield

    q = h[:, 0:GLA_WIDTH] * (GLA_DK ** -0.5)
    k = h[:, GLA_WIDTH:2 * GLA_WIDTH]
    qe = (q * jnp.exp(bcum - refb)).astype(BF16)
    ke = (k * jnp.exp(refb - bcum)).astype(BF16)
    qd = (q * jnp.exp(bcum)).astype(BF16)
    kd = (k * jnp.exp(lastb - bcum)).astype(BF16)
    vb = h[:, 2 * GLA_WIDTH:3 * GLA_WIDTH].astype(BF16)
    yield

    bd = _block_diag_mask(GLA_WIDTH, GLA_DK)
    bd16 = bdm_ref[...] > 0
    zero16 = jnp.zeros((GLA_WIDTH, GLA_WIDTH), BF16)
    ri4 = lax.broadcasted_iota(jnp.int32, (CHUNK, GLA_WIDTH), 0)
    ci4 = lax.broadcasted_iota(jnp.int32, (CHUNK, GLA_WIDTH), 1) % CHUNK
    causal4 = ci4 <= ri4

    state = state_ref[...]
    for c in range(t // CHUNK):
        rows = slice(c * CHUNK, (c + 1) * CHUNK)
        kstack = jnp.where(bd16, jnp.concatenate([ke[rows]] * GLA_HEADS, axis=0), zero16)
        vstack = jnp.where(bd16, jnp.concatenate([vb[rows]] * GLA_HEADS, axis=0), zero16)
        attn = _dot_nt(qe[rows], kstack)
        attn = jnp.where(causal4, attn, 0.0).astype(BF16)
        o_ref[rows, :] = _dot(attn, vstack) + _dot_nt(qd[rows], state.astype(BF16))
        ut = _dot_tn(vb[rows], kd[rows])
        decay = jnp.exp(lastb[c * CHUNK:c * CHUNK + 1, :])
        state = state * decay + jnp.where(bd, ut, 0.0)
        if c % 2 == 1:
            yield
    state_ref[...] = state

    o = o_ref[...]
    group_mean = jnp.where(bd16, jnp.full_like(zero16, 1.0 / GLA_DK), zero16)
    ms = _dot((o * o).astype(BF16), group_mean)
    g = h[:, 3 * GLA_WIDTH:4 * GLA_WIDTH]
    y_ref[...] = (o * lax.rsqrt(ms + RMS_EPS) * ng_ref[...] * (g * jax.nn.sigmoid(g))).astype(BF16)


def _lru_rows(get_h, cw_ref, cb_ref, wa_ref, ba_ref, wi_ref, bi_ref, lam_ref,
              xtail, hcarry, xbuf, a_scr, u_scr, h_scr, hin_scr, y_ref):
    h = get_h()
    t = h.shape[0]
    n_groups = t // SUBLANES
    xbuf[0:SUBLANES, :] = xtail[...]
    xbuf[SUBLANES:SUBLANES + t, :] = h[:, 0:LRU_WIDTH]
    xc = cb_ref[...] + jnp.zeros((t, LRU_WIDTH), F32)
    for w in range(CONV_WIDTH):
        off = SUBLANES - (CONV_WIDTH - 1) + w
        xc = xc + cw_ref[w:w + 1, :] * xbuf[off:off + t, :]
    xtail[...] = xbuf[t:t + SUBLANES, :]
    yield

    xcb = xc.astype(BF16)
    r = jax.nn.sigmoid(_dot(xcb, wa_ref[...]) + ba_ref[...])
    ig = jax.nn.sigmoid(_dot(xcb, wi_ref[...]) + bi_ref[...])
    log_a = (-LRU_C) * r * _softplus(-lam_ref[...])
    a = jnp.exp(log_a)
    th = jnp.tanh(log_a)
    u = jnp.sqrt(-2.0 * th / (1.0 - th)) * (ig * xc)
    yield

    row_in_group = lax.broadcasted_iota(jnp.int32, (t, LRU_WIDTH), 0) % SUBLANES
    d = 1
    while d < SUBLANES:
        u = u + a * _shift_rows(u, d, 0.0, row_in_group)
        a = a * _shift_rows(a, d, 1.0, row_in_group)
        d *= 2
    lane_halves = [slice(v * LANES, (v + 1) * LANES) for v in range(LRU_WIDTH // LANES)]
    group_last = pl.ds(SUBLANES - 1, n_groups, stride=SUBLANES)
    for v, lanes in enumerate(lane_halves):
        a_scr[v] = a[:, lanes]
        u_scr[v] = u[:, lanes]
    yield
    a_g = jnp.concatenate([a_scr[v, group_last, :] for v in range(len(lane_halves))], axis=1)
    u_g = jnp.concatenate([u_scr[v, group_last, :] for v in range(len(lane_halves))], axis=1)
    group_idx = lax.broadcasted_iota(jnp.int32, (n_groups, LRU_WIDTH), 0)
    d = 1
    while d < n_groups:
        u_g = u_g + a_g * _shift_rows(u_g, d, 0.0, group_idx)
        a_g = a_g * _shift_rows(a_g, d, 1.0, group_idx)
        d *= 2
    h_prev = hcarry[0:1, :]
    h_end = u_g + a_g * h_prev
    hin_scr[...] = _shift_rows(h_end, 1, h_prev, group_idx)
    hcarry[...] = jnp.broadcast_to(h_end[n_groups - 1:n_groups, :], hcarry.shape)
    yield
    for g in range(n_groups):
        rows = slice(g * SUBLANES, (g + 1) * SUBLANES)
        for v, lanes in enumerate(lane_halves):
            h_scr[rows, lanes] = u_scr[v, rows, :] + a_scr[v, rows, :] * hin_scr[g:g + 1, lanes]
    y_ref[...] = (h_scr[...] * jax.nn.gelu(h[:, LRU_WIDTH:2 * LRU_WIDTH])).astype(BF16)


def _round_robin(*gens):
    pending = list(gens)
    while pending:
        alive = []
        for g in pending:
            if next(g, StopIteration) is not StopIteration:
                alive.append(g)
                yield
        pending = alive


def _mixer_in_kernel(x_ref, w_ref, wup_ref, bup_ref, ng_ref, bdm_ref,
                     cw_ref, cb_ref, wa_ref, ba_ref, wi_ref, bi_ref, lam_ref,
                     att_ref, yg_ref, yl_ref,
                     state_scr, xtail, hcarry, o_scr, xbuf, a_scr, u_scr, h_scr, hin_scr):
    @pl.when(pl.program_id(1) == 0)
    def _():
        state_scr[...] = jnp.zeros_like(state_scr)
        xtail[...] = jnp.zeros_like(xtail)
        hcarry[...] = jnp.zeros_like(hcarry)

    n_parts = ROW_TILE // SUB_ROWS
    projected = [dict() for _ in range(n_parts)]

    def projection(k):
        rows = slice(k * SUB_ROWS, (k + 1) * SUB_ROWS)
        xb = x_ref[rows, :].astype(BF16)
        out = projected[k]
        for name, off, width in (("lru", LRU_OFF, LRU_COLS), ("glr", GLR_OFF, GLR_COLS),
                                 ("gla", 0, GLA_COLS)):
            cols = []
            for c in range(0, width, PROJ_PIECE):
                n_cols = min(PROJ_PIECE, width - c)
                cols.append(_dot(xb, w_ref[:, off + c:off + c + n_cols]))
                yield
            out[name] = jnp.concatenate(cols, axis=1) if len(cols) > 1 else cols[0]
        for c in range(0, ATT_COLS, PROJ_PIECE):
            att_ref[rows, c:c + PROJ_PIECE] = _dot(
                xb, w_ref[:, ATT_OFF + c:ATT_OFF + c + PROJ_PIECE]).astype(BF16)
            yield

    def mixers(k):
        rows = slice(k * SUB_ROWS, (k + 1) * SUB_ROWS)
        out = projected[k]
        lru = _lru_rows(lambda: out["lru"], cw_ref, cb_ref, wa_ref, ba_ref, wi_ref, bi_ref, lam_ref,
                        xtail, hcarry, xbuf.at[k], a_scr.at[k], u_scr.at[k], h_scr.at[k],
                        hin_scr.at[k], yl_ref.at[rows])
        gla = _gla_rows(lambda: (out["gla"], out["glr"]), wup_ref, bup_ref, ng_ref, bdm_ref,
                        o_scr.at[k], state_scr, yg_ref.at[rows])
        return itertools.chain(lru, gla)

    for _ in projection(0):
        pass
    for k in range(n_parts):
        nxt = projection(k + 1) if k + 1 < n_parts else iter(())
        for _ in _round_robin(nxt, mixers(k)):
            pass


def _mixer_in(layer, x2, w_perm, wup_pad, bup, ng_row, conv_w, conv_b, wa_bd, b_a, wi_bd, b_i, lam,
              batch):
    n = x2.shape[0]
    nt = n // batch // ROW_TILE
    n_parts = ROW_TILE // SUB_ROWS
    d = np.arange(GLA_WIDTH) // GLA_DK
    bdm = jnp.asarray((d[:, None] == d[None, :]).astype(np.float32), BF16)
    row = lambda c: pl.BlockSpec((ROW_TILE, c), lambda b, i: (b * nt + i, 0))
    full = lambda r, c: pl.BlockSpec((r, c), lambda b, i: (0, 0))
    of_layer = lambda r, c: pl.BlockSpec((None, r, c), lambda b, i: (layer, 0, 0))
    part = lambda r, c: pltpu.VMEM((n_parts, r, c), F32)
    lane_halves = pltpu.VMEM((n_parts, LRU_WIDTH // LANES, SUB_ROWS, LANES), F32)
    return pl.pallas_call(
        _mixer_in_kernel,
        grid=(batch, nt),
        in_specs=[row(D_MODEL),
                  pl.BlockSpec((None, D_MODEL, D_IN_PAD), lambda b, i: (layer, 0, 0),
                               pipeline_mode=pl.Buffered(1)),
                  of_layer(GLR_COLS, GLA_WIDTH), of_layer(1, GLA_WIDTH), of_layer(1, GLA_WIDTH),
                  full(GLA_WIDTH, GLA_WIDTH),
                  of_layer(CONV_WIDTH, LRU_WIDTH), of_layer(1, LRU_WIDTH),
                  of_layer(LRU_WIDTH, LRU_WIDTH), of_layer(1, LRU_WIDTH),
                  of_layer(LRU_WIDTH, LRU_WIDTH), of_layer(1, LRU_WIDTH), of_layer(1, LRU_WIDTH)],
        out_specs=[row(ATT_COLS), row(GLA_WIDTH), row(LRU_WIDTH)],
        out_shape=[jax.ShapeDtypeStruct((n, ATT_COLS), BF16),
                   jax.ShapeDtypeStruct((n, GLA_WIDTH), BF16),
                   jax.ShapeDtypeStruct((n, LRU_WIDTH), BF16)],
        scratch_shapes=[pltpu.VMEM((GLA_WIDTH, GLA_WIDTH), F32),
                        pltpu.VMEM((SUBLANES, LRU_WIDTH), F32),
                        pltpu.VMEM((SUBLANES, LRU_WIDTH), F32),
                        part(SUB_ROWS, GLA_WIDTH),
                        part(SUB_ROWS + SUBLANES, LRU_WIDTH),
                        lane_halves, lane_halves,
                        part(SUB_ROWS, LRU_WIDTH),
                        part(SUB_ROWS // SUBLANES, LRU_WIDTH)],
        compiler_params=pltpu.CompilerParams(
            dimension_semantics=("arbitrary", "arbitrary"), vmem_limit_bytes=VMEM_LIMIT_BYTES),
        name="mixer_in",
    )(x2, w_perm, wup_pad, bup, ng_row, bdm, conv_w, conv_b, wa_bd, b_a, wi_bd, b_i, lam)


def _att_kernel(q_ref, kw_ref, vw_ref, diag_ref, hm_ref, y_ref, bias_ref):
    lane_head0 = lax.broadcasted_iota(jnp.int32, (ATT_SUB, LANES), 1) < ATT_DH

    @pl.when((pl.program_id(1) == 0) & (pl.program_id(2) == 0))
    def _():
        r = lax.broadcasted_iota(jnp.int32, (ATT_SUB, ATT_WIN), 0)
        w = lax.broadcasted_iota(jnp.int32, (ATT_SUB, ATT_WIN), 1)
        kk = w - (r // CHUNK) * CHUNK
        in_band = (kk >= 0) & (kk < (ATT_LEFT_CHUNKS + 1) * CHUNK)
        for hh in range(LANES // ATT_DH):
            rows = jnp.broadcast_to(diag_ref[hh:hh + 1, :], (ATT_SUB, ATT_DIAG))
            skew = pltpu.roll(rows, ATT_DIAG - (ATT_SUB - 1), axis=1, stride=1, stride_axis=0)
            bias_ref[hh] = jnp.where(in_band, skew[:, 0:ATT_WIN], NEG_INF)

    def body(first):
        n_keys = ROW_TILE if first else 2 * ROW_TILE
        head0 = hm_ref[0:n_keys, :] > 0
        v = vw_ref[0:n_keys, :]
        one16 = jnp.ones_like(v)
        vh = (jnp.where(head0, v, one16), jnp.where(head0, one16, v))
        q = q_ref[...]
        zero16 = jnp.zeros_like(q)
        qh = (jnp.where(head0[0:ROW_TILE], q, zero16), jnp.where(head0[0:ROW_TILE], zero16, q))

        def window(j):
            if first:
                return slice(0, (j + 1) * ATT_SUB)
            return slice(j * ATT_SUB, j * ATT_SUB + ATT_WIN)

        def scores(j, hh):
            win = window(j)
            n_win = win.stop - win.start
            rows = slice(j * ATT_SUB, (j + 1) * ATT_SUB)
            return (_dot_nt(qh[hh][rows], kw_ref[win, :])
                    + bias_ref[hh, :, ATT_WIN - n_win:ATT_WIN])

        units = [(j, hh) for j in range(ROW_TILE // ATT_SUB) for hh in range(LANES // ATT_DH)]
        n_units = len(units)
        s, p, outs = {}, {}, []
        for step in range(n_units + 2):
            if step < n_units:
                s[step] = scores(*units[step])
            if step >= 2:
                j, hh = units[step - 2]
                of = _dot(p.pop(step - 2), vh[hh][window(j)])
                outs.append(of / pltpu.roll(of, ATT_DH, axis=1))
                if hh == LANES // ATT_DH - 1:
                    rows = slice(j * ATT_SUB, (j + 1) * ATT_SUB)
                    y_ref[rows, :] = jnp.where(lane_head0, outs[0], outs[1]).astype(BF16)
                    outs = []
            if 1 <= step <= n_units:
                su = s.pop(step - 1)
                m = jnp.max(su, axis=-1, keepdims=True)
                p[step - 1] = jnp.exp2(su - m).astype(BF16)

    @pl.when(pl.program_id(2) == 0)
    def _():
        body(True)

    @pl.when(pl.program_id(2) > 0)
    def _():
        body(False)


def _attention(att_h, diag, batch):
    n = att_h.shape[0]
    seq = n // batch
    att3 = att_h.reshape(batch, seq, ATT_COLS)
    pairs = ATT_WIDTH // LANES
    head0 = jnp.asarray(
        np.broadcast_to(np.arange(LANES) < ATT_DH, (2 * ROW_TILE, LANES)).astype(np.float32), BF16)
    window = lambda off: pl.BlockSpec(
        (None, pl.Element(2 * ROW_TILE), pl.Element(LANES)),
        lambda p, b, i: (b, jnp.maximum(i - 1, 0) * ROW_TILE, (off + p) * LANES))
    y = pl.pallas_call(
        _att_kernel,
        grid=(pairs, batch, seq // ROW_TILE),
        in_specs=[pl.BlockSpec((None, ROW_TILE, LANES), lambda p, b, i: (b, i, p)),
                  window(pairs), window(2 * pairs),
                  pl.BlockSpec((None, LANES // ATT_DH, ATT_DIAG), lambda p, b, i: (p, 0, 0)),
                  pl.BlockSpec((2 * ROW_TILE, LANES), lambda p, b, i: (0, 0))],
        out_specs=pl.BlockSpec((None, ROW_TILE, LANES), lambda p, b, i: (b, i, p)),
        out_shape=jax.ShapeDtypeStruct((batch, seq, ATT_WIDTH), BF16),
        scratch_shapes=[pltpu.VMEM((LANES // ATT_DH, ATT_SUB, ATT_WIN), F32)],
        compiler_params=pltpu.CompilerParams(
            dimension_semantics=("arbitrary", "arbitrary", "arbitrary"),
            vmem_limit_bytes=VMEM_LIMIT_BYTES),
        name="band_attention",
    )(att3, att3, att3, diag, head0)
    return y.reshape(n, ATT_WIDTH)


def _attention_diag(rel_bias):
    n_diag = ATT_SUB + ATT_WIN - 1
    rel = ATT_LEFT_CHUNKS * CHUNK + (ATT_SUB - 1) - np.arange(n_diag)
    idx = np.clip(rel, -REL_CLIP, REL_CLIP) + REL_CLIP
    diag = jnp.pad(rel_bias.astype(F32)[:, idx] * LOG2_E, ((0, 0), (0, ATT_DIAG - n_diag)))
    return diag.reshape(ATT_WIDTH // LANES, LANES // ATT_DH, ATT_DIAG)


def _post_kernel(yg_ref, ya_ref, yl_ref, x_ref, p_ref, wo_ref, g1_ref, b1_ref,
                 wg_ref, wu_ref, wd_ref, wpg_ref, wpp_ref, g2_ref, b2_ref, o_ref, act_ref):
    n_parts = POST_TILE // POST_ROWS
    rows = [slice(k * POST_ROWS, (k + 1) * POST_ROWS) for k in range(n_parts)]
    n_ff = D_FF // FF_TILE
    mix, x1, xb, ffn, ple = ({} for _ in range(5))

    def out_projection(k):
        mix[k] = (_dot(yg_ref[rows[k], :], wo_ref[0:GLA_WIDTH, :])
                  + _dot(ya_ref[rows[k], :], wo_ref[GLA_WIDTH:GLA_WIDTH + ATT_WIDTH, :])
                  + _dot(yl_ref[rows[k], :], wo_ref[GLA_WIDTH + ATT_WIDTH:D_MODEL, :]))

    def norm1(k):
        x1[k] = _layer_norm(DN_ALPHA * x_ref[rows[k], :] + mix.pop(k), g1_ref[...], b1_ref[...])
        xb[k] = x1[k].astype(BF16)

    def ffn_up(k, c):
        cols = slice(c * FF_TILE, (c + 1) * FF_TILE)
        gate = _dot(xb[k], wg_ref[:, cols])
        up = _dot(xb[k], wu_ref[:, cols])
        act_ref[k, :, cols] = (gate * jax.nn.sigmoid(gate) * up).astype(BF16)

    def ffn_down(k):
        ffn[k] = _dot(act_ref[k], wd_ref[...])
        ple[k] = (jax.nn.sigmoid(_dot(xb.pop(k), wpg_ref[...]))
                  * _dot(p_ref[rows[k], :].astype(BF16), wpp_ref[...]))

    def norm2(k):
        o_ref[rows[k], :] = _layer_norm(DN_ALPHA * x1.pop(k) + ffn.pop(k) + ple.pop(k),
                                        g2_ref[...], b2_ref[...])

    for k in range(n_parts):
        out_projection(k)
    norm1(0)
    for k in range(n_parts):
        for c in range(n_ff):
            ffn_up(k, c)
            if c == n_ff // 2 and k + 1 < n_parts:
                norm1(k + 1)
    for k in range(n_parts):
        ffn_down(k)
        if k > 0:
            norm2(k - 1)
    norm2(n_parts - 1)


def _post(layer, y_gla, y_att, y_lru, x2, p3, wo, g1, b1, wg, wu, wd, wpg, wpp, g2, b2):
    n = x2.shape[0]
    row = lambda c: pl.BlockSpec((POST_TILE, c), lambda i: (i, 0))
    full = lambda r, c: pl.BlockSpec((None, r, c), lambda i: (layer, 0, 0),
                                     pipeline_mode=pl.Buffered(1))
    return pl.pallas_call(
        _post_kernel,
        grid=(n // POST_TILE,),
        in_specs=[row(GLA_WIDTH), row(ATT_WIDTH), row(LRU_WIDTH), row(D_MODEL),
                  pl.BlockSpec((None, POST_TILE, PLE_DIM), lambda i: (layer, i, 0)),
                  full(D_MODEL, D_MODEL), full(1, D_MODEL), full(1, D_MODEL),
                  full(D_MODEL, D_FF), full(D_MODEL, D_FF), full(D_FF, D_MODEL),
                  full(D_MODEL, D_MODEL), full(PLE_DIM, D_MODEL),
                  full(1, D_MODEL), full(1, D_MODEL)],
        out_specs=row(D_MODEL),
        out_shape=jax.ShapeDtypeStruct((n, D_MODEL), F32),
        scratch_shapes=[pltpu.VMEM((POST_TILE // POST_ROWS, POST_ROWS, D_FF), BF16)],
        compiler_params=pltpu.CompilerParams(
            dimension_semantics=("arbitrary",), vmem_limit_bytes=VMEM_LIMIT_BYTES),
        name="out_proj_ffn",
    )(y_gla, y_att, y_lru, x2, p3, wo, g1, b1, wg, wu, wd, wpg, wpp, g2, b2)


def _block_diag(w):
    n_layers, g, c, d = w.shape
    eye = jnp.eye(g, dtype=w.dtype)
    return (eye[None, :, None, :, None] * w[:, :, :, None, :]).reshape(n_layers, g * c, g * d)


def _permute_w_in(w):
    glr_lo = GLA_COLS
    glr_hi = GLA_COLS + GLA_GATE_RANK
    aq_hi = glr_hi + ATT_WIDTH
    pad = jnp.zeros(w.shape[:2] + (GLR_COLS - GLA_GATE_RANK,), w.dtype)
    return jnp.concatenate(
        [w[..., :glr_lo], w[..., glr_hi:aq_hi] * (ATT_DH ** -0.5 * LOG2_E), w[..., aq_hi:],
         w[..., glr_lo:glr_hi], pad], axis=-1).astype(BF16)


def kernel(x, p, w_in, gla_w_lr_up, gla_b_lr_up, gla_norm_g, rel_bias, lru_conv_w, lru_conv_b,
           lru_w_a, lru_b_a, lru_w_i, lru_b_i, lru_lambda, w_out, ln1_g, ln1_b, w_ffn_gate,
           w_ffn_up, w_ffn_down, w_ple_gate, w_ple_proj, ln2_g, ln2_b):
    batch, seq, _ = x.shape
    n = batch * seq
    assert seq % ROW_TILE == 0
    x2 = x.reshape(n, D_MODEL)
    att_diag = _attention_diag(rel_bias)
    rows = lambda a: a.reshape(DEPTH, 1, -1).astype(F32)
    bf16 = lambda a: a.astype(BF16)

    mixer_params = (
        _permute_w_in(w_in),
        bf16(jnp.pad(gla_w_lr_up, ((0, 0), (0, GLR_COLS - GLA_GATE_RANK), (0, 0)))),
        rows(gla_b_lr_up), rows(jnp.tile(gla_norm_g, (1, GLA_HEADS))),
        lru_conv_w.astype(F32), rows(lru_conv_b),
        bf16(_block_diag(lru_w_a)), rows(lru_b_a),
        bf16(_block_diag(lru_w_i)), rows(lru_b_i), rows(lru_lambda))
    post_params = (
        bf16(w_out), rows(ln1_g), rows(ln1_b), bf16(w_ffn_gate), bf16(w_ffn_up), bf16(w_ffn_down),
        bf16(w_ple_gate), bf16(w_ple_proj), rows(ln2_g), rows(ln2_b))
    p3 = p.reshape(DEPTH, n, PLE_DIM)

    for layer in range(DEPTH):
        att_h, y_gla, y_lru = _mixer_in(layer, x2, *mixer_params, batch)
        y_att = _attention(att_h, att_diag, batch)
        x2 = _post(layer, y_gla, y_att, y_lru, x2, p3, *post_params)
    return x2.reshape(batch, seq, D_MODEL)
```

```python
import itertools
import math

import numpy as np
import jax
import jax.numpy as jnp
from jax import lax
from jax.experimental import pallas as pl
from jax.experimental.pallas import tpu as pltpu

F32 = jnp.float32
BF16 = jnp.bfloat16

D_MODEL = 1024
CHUNK = 64
GLA_HEADS = 4
GLA_DK = 64
GLA_WIDTH = 256
GLA_GATE_RANK = 16
GLA_GATE_TAU = 16.0
ATT_HEADS = 8
ATT_DH = 64
ATT_WIDTH = 512
ATT_LEFT_CHUNKS = 8
REL_CLIP = 256
LRU_WIDTH = 256
LRU_BLOCKS = 4
LRU_C = 8.0
CONV_WIDTH = 4
D_FF = 2816
PLE_DIM = 256
DEPTH = 2
DN_ALPHA = (2 * DEPTH) ** 0.25
LN_EPS = 1e-5
RMS_EPS = 1e-6
NEG_INF = -1e30
LOG2_E = math.log2(math.e)

LANES = 128
SUBLANES = 8
VMEM_LIMIT_BYTES = 56 * 1024 * 1024

GLA_COLS = 4 * GLA_WIDTH
ATT_COLS = 3 * ATT_WIDTH
LRU_COLS = 2 * LRU_WIDTH
GLR_COLS = LANES
ATT_OFF = GLA_COLS
LRU_OFF = ATT_OFF + ATT_COLS
GLR_OFF = LRU_OFF + LRU_COLS
D_IN_PAD = GLR_OFF + GLR_COLS

ROW_TILE = 512
ATT_SUB = 256
ATT_WIN = ATT_SUB + ATT_LEFT_CHUNKS * CHUNK
ATT_DIAG = ATT_SUB + ATT_WIN
HEADS_PER_PAIR = LANES // ATT_DH
ATT_PAIRS = 4
FF_TILE = 256
PROJ_PIECE = 256
SUB_ROWS = 512
POST_TILE = 512
POST_ROWS = 256


def _dot(a, b):
    return jnp.dot(a, b, preferred_element_type=F32)


def _dot_nt(a, b):
    return lax.dot_general(a, b, (((1,), (1,)), ((), ())), preferred_element_type=F32)


def _dot_tn(a, b):
    return lax.dot_general(a, b, (((0,), (0,)), ((), ())), preferred_element_type=F32)


def _block_diag_mask(n, blk):
    r = lax.broadcasted_iota(jnp.int32, (n, n), 0) // blk
    c = lax.broadcasted_iota(jnp.int32, (n, n), 1) // blk
    return r == c


def _layer_norm(z, g, b):
    mu = jnp.mean(z, axis=-1, keepdims=True)
    zc = z - mu
    var = jnp.mean(zc * zc, axis=-1, keepdims=True)
    return zc * lax.rsqrt(var + LN_EPS) * g + b


def _softplus(x):
    return jnp.maximum(x, 0.0) + jnp.log1p(jnp.exp(-jnp.abs(x)))


def _shift_rows(x, d, fill, row_idx):
    return jnp.where(row_idx >= d, pltpu.roll(x, d, axis=0), fill)


def _gla_rows(get_h, wup_ref, bup_ref, ng_ref, bdm_ref, o_ref, state_ref, y_ref):
    h, lr = get_h()
    t = h.shape[0]
    z = _dot(lr.astype(BF16), wup_ref[...]) + bup_ref[...]
    la = -_softplus(-z) * (1.0 / GLA_GATE_TAU)
    la_hi = la.astype(BF16)
    la_lo = (la - la_hi.astype(F32)).astype(BF16)
    ri = lax.broadcasted_iota(jnp.int32, (CHUNK, CHUNK), 0)
    ci = lax.broadcasted_iota(jnp.int32, (CHUNK, CHUNK), 1)
    tri = jnp.where(ci <= ri, 1.0, 0.0).astype(BF16)
    bcum, refb, lastb = [], [], []
    for c in range(t // CHUNK):
        rows = slice(c * CHUNK, (c + 1) * CHUNK)
        bc = _dot(tri, la_hi[rows]) + _dot(tri, la_lo[rows])
        bcum.append(bc)
        refb.append(jnp.broadcast_to(bc[CHUNK // 2:CHUNK // 2 + 1], bc.shape))
        lastb.append(jnp.broadcast_to(bc[CHUNK - 1:CHUNK], bc.shape))
    bcum = jnp.concatenate(bcum, axis=0)
    refb = jnp.concatenate(refb, axis=0)
    lastb = jnp.concatenate(lastb, axis=0)
    yield

    q = h[:, 0:GLA_WIDTH] * (GLA_DK ** -0.5)
    k = h[:, GLA_WIDTH:2 * GLA_WIDTH]
    qe = (q * jnp.exp(bcum - refb)).astype(BF16)
    ke = (k * jnp.exp(refb - bcum)).astype(BF16)
    qd = (q * jnp.exp(bcum)).astype(BF16)
    kd = (k * jnp.exp(lastb - bcum)).astype(BF16)
    vb = h[:, 2 * GLA_WIDTH:3 * GLA_WIDTH].astype(BF16)
    yield

    bd = _block_diag_mask(GLA_WIDTH, GLA_DK)
    bd16 = bdm_ref[...] > 0
    zero16 = jnp.zeros((GLA_WIDTH, GLA_WIDTH), BF16)
    ri4 = lax.broadcasted_iota(jnp.int32, (CHUNK, GLA_WIDTH), 0)
    ci4 = lax.broadcasted_iota(jnp.int32, (CHUNK, GLA_WIDTH), 1) % CHUNK
    causal4 = ci4 <= ri4

    state = state_ref[...]
    for c in range(t // CHUNK):
        rows = slice(c * CHUNK, (c + 1) * CHUNK)
        kstack = jnp.where(bd16, jnp.concatenate([ke[rows]] * GLA_HEADS, axis=0), zero16)
        vstack = jnp.where(bd16, jnp.concatenate([vb[rows]] * GLA_HEADS, axis=0), zero16)
        attn = _dot_nt(qe[rows], kstack)
        attn = jnp.where(causal4, attn, 0.0).astype(BF16)
        o_ref[rows, :] = _dot(attn, vstack) + _dot_nt(qd[rows], state.astype(BF16))
        ut = _dot_tn(vb[rows], kd[rows])
        decay = jnp.exp(lastb[c * CHUNK:c * CHUNK + 1, :])
        state = state * decay + jnp.where(bd, ut, 0.0)
        if c % 2 == 1:
            yield
    state_ref[...] = state

    o = o_ref[...]
    group_mean = jnp.where(bd16, jnp.full_like(zero16, 1.0 / GLA_DK), zero16)
    ms = _dot((o * o).astype(BF16), group_mean)
    g = h[:, 3 * GLA_WIDTH:4 * GLA_WIDTH]
    y_ref[...] = (o * lax.rsqrt(ms + RMS_EPS) * ng_ref[...] * (g * jax.nn.sigmoid(g))).astype(BF16)


def _lru_rows(get_h, cw_ref, cb_ref, wa_ref, ba_ref, wi_ref, bi_ref, lam_ref,
              xtail, hcarry, xbuf, a_scr, u_scr, h_scr, hin_scr, y_ref):
    h = get_h()
    t = h.shape[0]
    n_groups = t // SUBLANES
    xbuf[0:SUBLANES, :] = xtail[...]
    xbuf[SUBLANES:SUBLANES + t, :] = h[:, 0:LRU_WIDTH]
    xc = cb_ref[...] + jnp.zeros((t, LRU_WIDTH), F32)
    for w in range(CONV_WIDTH):
        off = SUBLANES - (CONV_WIDTH - 1) + w
        xc = xc + cw_ref[w:w + 1, :] * xbuf[off:off + t, :]
    xtail[...] = xbuf[t:t + SUBLANES, :]
    yield

    xcb = xc.astype(BF16)
    r = jax.nn.sigmoid(_dot(xcb, wa_ref[...]) + ba_ref[...])
    ig = jax.nn.sigmoid(_dot(xcb, wi_ref[...]) + bi_ref[...])
    log_a = (-LRU_C) * r * _softplus(-lam_ref[...])
    a = jnp.exp(log_a)
    th = jnp.tanh(log_a)
    u = jnp.sqrt(-2.0 * th / (1.0 - th)) * (ig * xc)
    yield

    row_in_group = lax.broadcasted_iota(jnp.int32, (t, LRU_WIDTH), 0) % SUBLANES
    d = 1
    while d < SUBLANES:
        u = u + a * _shift_rows(u, d, 0.0, row_in_group)
        a = a * _shift_rows(a, d, 1.0, row_in_group)
        d *= 2
    lane_halves = [slice(v * LANES, (v + 1) * LANES) for v in range(LRU_WIDTH // LANES)]
    group_last = pl.ds(SUBLANES - 1, n_groups, stride=SUBLANES)
    for v, lanes in enumerate(lane_halves):
        a_scr[v] = a[:, lanes]
        u_scr[v] = u[:, lanes]
    yield
    a_g = jnp.concatenate([a_scr[v, group_last, :] for v in range(len(lane_halves))], axis=1)
    u_g = jnp.concatenate([u_scr[v, group_last, :] for v in range(len(lane_halves))], axis=1)
    group_idx = lax.broadcasted_iota(jnp.int32, (n_groups, LRU_WIDTH), 0)
    d = 1
    while d < n_groups:
        u_g = u_g + a_g * _shift_rows(u_g, d, 0.0, group_idx)
        a_g = a_g * _shift_rows(a_g, d, 1.0, group_idx)
        d *= 2
    h_prev = hcarry[0:1, :]
    h_end = u_g + a_g * h_prev
    hin_scr[...] = _shift_rows(h_end, 1, h_prev, group_idx)
    hcarry[...] = jnp.broadcast_to(h_end[n_groups - 1:n_groups, :], hcarry.shape)
    yield
    for g in range(n_groups):
        rows = slice(g * SUBLANES, (g + 1) * SUBLANES)
        for v, lanes in enumerate(lane_halves):
            h_scr[rows, lanes] = u_scr[v, rows, :] + a_scr[v, rows, :] * hin_scr[g:g + 1, lanes]
    y_ref[...] = (h_scr[...] * jax.nn.gelu(h[:, LRU_WIDTH:2 * LRU_WIDTH])).astype(BF16)


def _round_robin(*gens):
    pending = list(gens)
    while pending:
        alive = []
        for g in pending:
            if next(g, StopIteration) is not StopIteration:
                alive.append(g)
                yield
        pending = alive


def _mixer_in_kernel(x_ref, w_ref, wup_ref, bup_ref, ng_ref, bdm_ref,
                     cw_ref, cb_ref, wa_ref, ba_ref, wi_ref, bi_ref, lam_ref,
                     att_ref, yg_ref, yl_ref,
                     state_scr, xtail, hcarry, o_scr, xbuf, a_scr, u_scr, h_scr, hin_scr):
    @pl.when(pl.program_id(1) == 0)
    def _():
        state_scr[...] = jnp.zeros_like(state_scr)
        xtail[...] = jnp.zeros_like(xtail)
        hcarry[...] = jnp.zeros_like(hcarry)

    n_parts = ROW_TILE // SUB_ROWS
    projected = [dict() for _ in range(n_parts)]

    def projection(k):
        rows = slice(k * SUB_ROWS, (k + 1) * SUB_ROWS)
        xb = x_ref[rows, :].astype(BF16)
        out = projected[k]
        for name, off, width in (("lru", LRU_OFF, LRU_COLS), ("glr", GLR_OFF, GLR_COLS),
                                 ("gla", 0, GLA_COLS)):
            cols = []
            for c in range(0, width, PROJ_PIECE):
                n_cols = min(PROJ_PIECE, width - c)
                cols.append(_dot(xb, w_ref[:, off + c:off + c + n_cols]))
                yield
            out[name] = jnp.concatenate(cols, axis=1) if len(cols) > 1 else cols[0]
        for c in range(0, ATT_COLS, PROJ_PIECE):
            att_ref[rows, c:c + PROJ_PIECE] = _dot(
                xb, w_ref[:, ATT_OFF + c:ATT_OFF + c + PROJ_PIECE]).astype(BF16)
            yield

    def mixers(k):
        rows = slice(k * SUB_ROWS, (k + 1) * SUB_ROWS)
        out = projected[k]
        lru = _lru_rows(lambda: out["lru"], cw_ref, cb_ref, wa_ref, ba_ref, wi_ref, bi_ref, lam_ref,
                        xtail, hcarry, xbuf.at[k], a_scr.at[k], u_scr.at[k], h_scr.at[k],
                        hin_scr.at[k], yl_ref.at[rows])
        gla = _gla_rows(lambda: (out["gla"], out["glr"]), wup_ref, bup_ref, ng_ref, bdm_ref,
                        o_scr.at[k], state_scr, yg_ref.at[rows])
        return itertools.chain(lru, gla)

    for _ in projection(0):
        pass
    for k in range(n_parts):
        nxt = projection(k + 1) if k + 1 < n_parts else iter(())
        for _ in _round_robin(nxt, mixers(k)):
            pass


def _mixer_in(layer, x2, w_perm, wup_pad, bup, ng_row, conv_w, conv_b, wa_bd, b_a, wi_bd, b_i, lam,
              batch):
    n = x2.shape[0]
    nt = n // batch // ROW_TILE
    n_parts = ROW_TILE // SUB_ROWS
    d = np.arange(GLA_WIDTH) // GLA_DK
    bdm = jnp.asarray((d[:, None] == d[None, :]).astype(np.float32), BF16)
    row = lambda c: pl.BlockSpec((ROW_TILE, c), lambda b, i: (b * nt + i, 0))
    full = lambda r, c: pl.BlockSpec((r, c), lambda b, i: (0, 0))
    of_layer = lambda r, c: pl.BlockSpec((None, r, c), lambda b, i: (layer, 0, 0))
    part = lambda r, c: pltpu.VMEM((n_parts, r, c), F32)
    lane_halves = pltpu.VMEM((n_parts, LRU_WIDTH // LANES, SUB_ROWS, LANES), F32)
    return pl.pallas_call(
        _mixer_in_kernel,
        grid=(batch, nt),
        in_specs=[row(D_MODEL),
                  pl.BlockSpec((None, D_MODEL, D_IN_PAD), lambda b, i: (layer, 0, 0),
                               pipeline_mode=pl.Buffered(1)),
                  of_layer(GLR_COLS, GLA_WIDTH), of_layer(1, GLA_WIDTH), of_layer(1, GLA_WIDTH),
                  full(GLA_WIDTH, GLA_WIDTH),
                  of_layer(CONV_WIDTH, LRU_WIDTH), of_layer(1, LRU_WIDTH),
                  of_layer(LRU_WIDTH, LRU_WIDTH), of_layer(1, LRU_WIDTH),
                  of_layer(LRU_WIDTH, LRU_WIDTH), of_layer(1, LRU_WIDTH), of_layer(1, LRU_WIDTH)],
        out_specs=[row(ATT_COLS), row(GLA_WIDTH), row(LRU_WIDTH)],
        out_shape=[jax.ShapeDtypeStruct((n, ATT_COLS), BF16),
                   jax.ShapeDtypeStruct((n, GLA_WIDTH), BF16),
                   jax.ShapeDtypeStruct((n, LRU_WIDTH), BF16)],
        scratch_shapes=[pltpu.VMEM((GLA_WIDTH, GLA_WIDTH), F32),
                        pltpu.VMEM((SUBLANES, LRU_WIDTH), F32),
                        pltpu.VMEM((SUBLANES, LRU_WIDTH), F32),
                        part(SUB_ROWS, GLA_WIDTH),
                        part(SUB_ROWS + SUBLANES, LRU_WIDTH),
                        lane_halves, lane_halves,
                        part(SUB_ROWS, LRU_WIDTH),
                        part(SUB_ROWS // SUBLANES, LRU_WIDTH)],
        compiler_params=pltpu.CompilerParams(
            dimension_semantics=("arbitrary", "arbitrary"), vmem_limit_bytes=VMEM_LIMIT_BYTES),
        name="mixer_in",
    )(x2, w_perm, wup_pad, bup, ng_row, bdm, conv_w, conv_b, wa_bd, b_a, wi_bd, b_i, lam)


def _att_kernel(q_ref, kw_ref, vw_ref, diag_ref, hm_ref, y_ref, bias_ref):
    lane_head0 = lax.broadcasted_iota(jnp.int32, (ATT_SUB, LANES), 1) < ATT_DH

    @pl.when((pl.program_id(1) == 0) & (pl.program_id(2) == 0))
    def _():
        r = lax.broadcasted_iota(jnp.int32, (ATT_SUB, ATT_WIN), 0)
        w = lax.broadcasted_iota(jnp.int32, (ATT_SUB, ATT_WIN), 1)
        kk = w - (r // CHUNK) * CHUNK
        in_band = (kk >= 0) & (kk < (ATT_LEFT_CHUNKS + 1) * CHUNK)
        for hh in range(ATT_PAIRS * HEADS_PER_PAIR):
            rows = jnp.broadcast_to(diag_ref[hh:hh + 1, :], (ATT_SUB, ATT_DIAG))
            skew = pltpu.roll(rows, ATT_DIAG - (ATT_SUB - 1), axis=1, stride=1, stride_axis=0)
            bias_ref[hh] = jnp.where(in_band, skew[:, 0:ATT_WIN], NEG_INF)

    def body(first):
        n_keys = ROW_TILE if first else 2 * ROW_TILE
        head0 = hm_ref[0:n_keys, :] > 0
        vh, qh = [], []
        for pair in range(ATT_PAIRS):
            lanes = slice(pair * LANES, (pair + 1) * LANES)
            v = vw_ref[0:n_keys, lanes]
            one16 = jnp.ones_like(v)
            vh.append((jnp.where(head0, v, one16), jnp.where(head0, one16, v)))
            q = q_ref[:, lanes]
            zero16 = jnp.zeros_like(q)
            qh.append((jnp.where(head0[0:ROW_TILE], q, zero16),
                       jnp.where(head0[0:ROW_TILE], zero16, q)))

        def window(j):
            if first:
                return slice(0, (j + 1) * ATT_SUB)
            return slice(j * ATT_SUB, j * ATT_SUB + ATT_WIN)

        def scores(pair, j, hh):
            win = window(j)
            n_win = win.stop - win.start
            rows = slice(j * ATT_SUB, (j + 1) * ATT_SUB)
            return (_dot_nt(qh[pair][hh][rows], kw_ref[win, pair * LANES:(pair + 1) * LANES])
                    + bias_ref[pair * HEADS_PER_PAIR + hh, :, ATT_WIN - n_win:ATT_WIN])

        units = [(pair, j, hh) for pair in range(ATT_PAIRS)
                 for j in range(ROW_TILE // ATT_SUB) for hh in range(HEADS_PER_PAIR)]
        n_units = len(units)
        s, p, outs = {}, {}, []
        for step in range(n_units + 2):
            if step < n_units:
                s[step] = scores(*units[step])
            if step >= 2:
                pair, j, hh = units[step - 2]
                of = _dot(p.pop(step - 2), vh[pair][hh][window(j)])
                outs.append(of / pltpu.roll(of, ATT_DH, axis=1))
                if hh == HEADS_PER_PAIR - 1:
                    rows = slice(j * ATT_SUB, (j + 1) * ATT_SUB)
                    y_ref[rows, pair * LANES:(pair + 1) * LANES] = jnp.where(
                        lane_head0, outs[0], outs[1]).astype(BF16)
                    outs = []
            if 1 <= step <= n_units:
                su = s.pop(step - 1)
                m = jnp.max(su, axis=-1, keepdims=True)
                p[step - 1] = jnp.exp2(su - m).astype(BF16)

    @pl.when(pl.program_id(2) == 0)
    def _():
        body(True)

    @pl.when(pl.program_id(2) > 0)
    def _():
        body(False)


def _attention(att_h, diag, batch):
    n = att_h.shape[0]
    seq = n // batch
    att3 = att_h.reshape(batch, seq, ATT_COLS)
    groups = ATT_WIDTH // (ATT_PAIRS * LANES)
    width = ATT_PAIRS * LANES
    head0 = jnp.asarray(
        np.broadcast_to(np.arange(LANES) < ATT_DH, (2 * ROW_TILE, LANES)).astype(np.float32), BF16)
    window = lambda off: pl.BlockSpec(
        (None, pl.Element(2 * ROW_TILE), pl.Element(width)),
        lambda g, b, i: (b, jnp.maximum(i - 1, 0) * ROW_TILE, (off + g * ATT_PAIRS) * LANES))
    y = pl.pallas_call(
        _att_kernel,
        grid=(groups, batch, seq // ROW_TILE),
        in_specs=[pl.BlockSpec((None, ROW_TILE, width), lambda g, b, i: (b, i, g)),
                  window(ATT_WIDTH // LANES), window(2 * ATT_WIDTH // LANES),
                  pl.BlockSpec((None, ATT_PAIRS * HEADS_PER_PAIR, ATT_DIAG),
                               lambda g, b, i: (g, 0, 0)),
                  pl.BlockSpec((2 * ROW_TILE, LANES), lambda g, b, i: (0, 0))],
        out_specs=pl.BlockSpec((None, ROW_TILE, width), lambda g, b, i: (b, i, g)),
        out_shape=jax.ShapeDtypeStruct((batch, seq, ATT_WIDTH), BF16),
        scratch_shapes=[pltpu.VMEM((ATT_PAIRS * HEADS_PER_PAIR, ATT_SUB, ATT_WIN), F32)],
        compiler_params=pltpu.CompilerParams(
            dimension_semantics=("arbitrary", "arbitrary", "arbitrary"),
            vmem_limit_bytes=VMEM_LIMIT_BYTES),
        name="band_attention",
    )(att3, att3, att3, diag, head0)
    return y.reshape(n, ATT_WIDTH)


def _attention_diag(rel_bias):
    n_diag = ATT_SUB + ATT_WIN - 1
    rel = ATT_LEFT_CHUNKS * CHUNK + (ATT_SUB - 1) - np.arange(n_diag)
    idx = np.clip(rel, -REL_CLIP, REL_CLIP) + REL_CLIP
    diag = jnp.pad(rel_bias.astype(F32)[:, idx] * LOG2_E, ((0, 0), (0, ATT_DIAG - n_diag)))
    return diag.reshape(-1, ATT_PAIRS * HEADS_PER_PAIR, ATT_DIAG)


def _post_kernel(yg_ref, ya_ref, yl_ref, x_ref, p_ref, wo_ref, g1_ref, b1_ref,
                 wg_ref, wu_ref, wd_ref, wpg_ref, wpp_ref, g2_ref, b2_ref, o_ref, act_ref):
    n_parts = POST_TILE // POST_ROWS
    rows = [slice(k * POST_ROWS, (k + 1) * POST_ROWS) for k in range(n_parts)]
    n_ff = D_FF // FF_TILE
    mix, x1, xb, ffn, ple = ({} for _ in range(5))

    def out_projection(k):
        mix[k] = (_dot(yg_ref[rows[k], :], wo_ref[0:GLA_WIDTH, :])
                  + _dot(ya_ref[rows[k], :], wo_ref[GLA_WIDTH:GLA_WIDTH + ATT_WIDTH, :])
                  + _dot(yl_ref[rows[k], :], wo_ref[GLA_WIDTH + ATT_WIDTH:D_MODEL, :]))

    def norm1(k):
        x1[k] = _layer_norm(DN_ALPHA * x_ref[rows[k], :] + mix.pop(k), g1_ref[...], b1_ref[...])
        xb[k] = x1[k].astype(BF16)

    def ffn_up(k, c):
        cols = slice(c * FF_TILE, (c + 1) * FF_TILE)
        gate = _dot(xb[k], wg_ref[:, cols])
        up = _dot(xb[k], wu_ref[:, cols])
        act_ref[k, :, cols] = (gate * jax.nn.sigmoid(gate) * up).astype(BF16)

    def ffn_down(k):
        ffn[k] = _dot(act_ref[k], wd_ref[...])
        ple[k] = (jax.nn.sigmoid(_dot(xb.pop(k), wpg_ref[...]))
                  * _dot(p_ref[rows[k], :].astype(BF16), wpp_ref[...]))

    def norm2(k):
        o_ref[rows[k], :] = _layer_norm(DN_ALPHA * x1.pop(k) + ffn.pop(k) + ple.pop(k),
                                        g2_ref[...], b2_ref[...])

    for k in range(n_parts):
        out_projection(k)
    norm1(0)
    for k in range(n_parts):
        for c in range(n_ff):
            ffn_up(k, c)
            if c == n_ff // 2 and k + 1 < n_parts:
                norm1(k + 1)
    for k in range(n_parts):
        ffn_down(k)
        if k > 0:
            norm2(k - 1)
    norm2(n_parts - 1)


def _post(layer, y_gla, y_att, y_lru, x2, p3, wo, g1, b1, wg, wu, wd, wpg, wpp, g2, b2):
    n = x2.shape[0]
    row = lambda c: pl.BlockSpec((POST_TILE, c), lambda i: (i, 0))
    full = lambda r, c: pl.BlockSpec((None, r, c), lambda i: (layer, 0, 0),
                                     pipeline_mode=pl.Buffered(1))
    return pl.pallas_call(
        _post_kernel,
        grid=(n // POST_TILE,),
        in_specs=[row(GLA_WIDTH), row(ATT_WIDTH), row(LRU_WIDTH), row(D_MODEL),
                  pl.BlockSpec((None, POST_TILE, PLE_DIM), lambda i: (layer, i, 0)),
                  full(D_MODEL, D_MODEL), full(1, D_MODEL), full(1, D_MODEL),
                  full(D_MODEL, D_FF), full(D_MODEL, D_FF), full(D_FF, D_MODEL),
                  full(D_MODEL, D_MODEL), full(PLE_DIM, D_MODEL),
                  full(1, D_MODEL), full(1, D_MODEL)],
        out_specs=row(D_MODEL),
        out_shape=jax.ShapeDtypeStruct((n, D_MODEL), F32),
        scratch_shapes=[pltpu.VMEM((POST_TILE // POST_ROWS, POST_ROWS, D_FF), BF16)],
        compiler_params=pltpu.CompilerParams(
            dimension_semantics=("arbitrary",), vmem_limit_bytes=VMEM_LIMIT_BYTES),
        name="out_proj_ffn",
    )(y_gla, y_att, y_lru, x2, p3, wo, g1, b1, wg, wu, wd, wpg, wpp, g2, b2)


def _block_diag(w):
    n_layers, g, c, d = w.shape
    eye = jnp.eye(g, dtype=w.dtype)
    return (eye[None, :, None, :, None] * w[:, :, :, None, :]).reshape(n_layers, g * c, g * d)


def _permute_w_in(w):
    glr_lo = GLA_COLS
    glr_hi = GLA_COLS + GLA_GATE_RANK
    aq_hi = glr_hi + ATT_WIDTH
    pad = jnp.zeros(w.shape[:2] + (GLR_COLS - GLA_GATE_RANK,), w.dtype)
    return jnp.concatenate(
        [w[..., :glr_lo], w[..., glr_hi:aq_hi] * (ATT_DH ** -0.5 * LOG2_E), w[..., aq_hi:],
         w[..., glr_lo:glr_hi], pad], axis=-1).astype(BF16)


def kernel(x, p, w_in, gla_w_lr_up, gla_b_lr_up, gla_norm_g, rel_bias, lru_conv_w, lru_conv_b,
           lru_w_a, lru_b_a, lru_w_i, lru_b_i, lru_lambda, w_out, ln1_g, ln1_b, w_ffn_gate,
           w_ffn_up, w_ffn_down, w_ple_gate, w_ple_proj, ln2_g, ln2_b):
    batch, seq, _ = x.shape
    n = batch * seq
    assert seq % ROW_TILE == 0
    x2 = x.reshape(n, D_MODEL)
    att_diag = _attention_diag(rel_bias)
    rows = lambda a: a.reshape(DEPTH, 1, -1).astype(F32)
    bf16 = lambda a: a.astype(BF16)

    mixer_params = (
        _permute_w_in(w_in),
        bf16(jnp.pad(gla_w_lr_up, ((0, 0), (0, GLR_COLS - GLA_GATE_RANK), (0, 0)))),
        rows(gla_b_lr_up), rows(jnp.tile(gla_norm_g, (1, GLA_HEADS))),
        lru_conv_w.astype(F32), rows(lru_conv_b),
        bf16(_block_diag(lru_w_a)), rows(lru_b_a),
        bf16(_block_diag(lru_w_i)), rows(lru_b_i), rows(lru_lambda))
    post_params = (
        bf16(w_out), rows(ln1_g), rows(ln1_b), bf16(w_ffn_gate), bf16(w_ffn_up), bf16(w_ffn_down),
        bf16(w_ple_gate), bf16(w_ple_proj), rows(ln2_g), rows(ln2_b))
    p3 = p.reshape(DEPTH, n, PLE_DIM)

    for layer in range(DEPTH):
        att_h, y_gla, y_lru = _mixer_in(layer, x2, *mixer_params, batch)
        y_att = _attention(att_h, att_diag, batch)
        x2 = _post(layer, y_gla, y_att, y_lru, x2, p3, *post_params)
    return x2.reshape(batch, seq, D_MODEL)
```

```python
import itertools
import math

import numpy as np
import jax
import jax.numpy as jnp
from jax import lax
from jax.experimental import pallas as pl
from jax.experimental.pallas import tpu as pltpu

F32 = jnp.float32
BF16 = jnp.bfloat16

D_MODEL = 1024
CHUNK = 64
GLA_HEADS = 4
GLA_DK = 64
GLA_WIDTH = 256
GLA_GATE_RANK = 16
GLA_GATE_TAU = 16.0
ATT_HEADS = 8
ATT_DH = 64
ATT_WIDTH = 512
ATT_LEFT_CHUNKS = 8
REL_CLIP = 256
LRU_WIDTH = 256
LRU_BLOCKS = 4
LRU_C = 8.0
CONV_WIDTH = 4
D_FF = 2816
PLE_DIM = 256
DEPTH = 2
DN_ALPHA = (2 * DEPTH) ** 0.25
LN_EPS = 1e-5
RMS_EPS = 1e-6
NEG_INF = -1e30
LOG2_E = math.log2(math.e)

LANES = 128
SUBLANES = 8
VMEM_LIMIT_BYTES = 56 * 1024 * 1024

GLA_COLS = 4 * GLA_WIDTH
ATT_COLS = 3 * ATT_WIDTH
LRU_COLS = 2 * LRU_WIDTH
GLR_COLS = LANES
ATT_OFF = GLA_COLS
LRU_OFF = ATT_OFF + ATT_COLS
GLR_OFF = LRU_OFF + LRU_COLS
D_IN_PAD = GLR_OFF + GLR_COLS

ROW_TILE = 512
MIX_TILE = 512
ATT_SUB = 256
ATT_WIN = ATT_SUB + ATT_LEFT_CHUNKS * CHUNK
ATT_DIAG = ATT_SUB + ATT_WIN
HEADS_PER_PAIR = LANES // ATT_DH
ATT_PAIRS = 4
FF_TILE = 256
PROJ_PIECE = 512
POST_TILE = 512
POST_ROWS = 256


def _dot(a, b):
    return jnp.dot(a, b, preferred_element_type=F32)


def _dot_nt(a, b):
    return lax.dot_general(a, b, (((1,), (1,)), ((), ())), preferred_element_type=F32)


def _dot_tn(a, b):
    return lax.dot_general(a, b, (((0,), (0,)), ((), ())), preferred_element_type=F32)


def _block_diag_mask(n, blk):
    r = lax.broadcasted_iota(jnp.int32, (n, n), 0) // blk
    c = lax.broadcasted_iota(jnp.int32, (n, n), 1) // blk
    return r == c


def _layer_norm(z, g, b):
    mu = jnp.mean(z, axis=-1, keepdims=True)
    zc = z - mu
    var = jnp.mean(zc * zc, axis=-1, keepdims=True)
    return zc * lax.rsqrt(var + LN_EPS) * g + b


def _softplus(x):
    return jnp.maximum(x, 0.0) + jnp.log1p(jnp.exp(-jnp.abs(x)))


def _shift_rows(x, d, fill, row_idx):
    return jnp.where(row_idx >= d, pltpu.roll(x, d, axis=0), fill)


def _gla_rows(get_h, wup_ref, bup_ref, ng_ref, bdm_ref, o_ref, state_ref, y_ref):
    h, lr = get_h()
    t = h.shape[0]
    z = _dot(lr.astype(BF16), wup_ref[...]) + bup_ref[...]
    la = -_softplus(-z) * (1.0 / GLA_GATE_TAU)
    la_hi = la.astype(BF16)
    la_lo = (la - la_hi.astype(F32)).astype(BF16)
    ri = lax.broadcasted_iota(jnp.int32, (CHUNK, CHUNK), 0)
    ci = lax.broadcasted_iota(jnp.int32, (CHUNK, CHUNK), 1)
    tri = jnp.where(ci <= ri, 1.0, 0.0).astype(BF16)
    bcum, refb, lastb = [], [], []
    for c in range(t // CHUNK):
        rows = slice(c * CHUNK, (c + 1) * CHUNK)
        bc = _dot(tri, la_hi[rows]) + _dot(tri, la_lo[rows])
        bcum.append(bc)
        refb.append(jnp.broadcast_to(bc[CHUNK // 2:CHUNK // 2 + 1], bc.shape))
        lastb.append(jnp.broadcast_to(bc[CHUNK - 1:CHUNK], bc.shape))
    bcum = jnp.concatenate(bcum, axis=0)
    refb = jnp.concatenate(refb, axis=0)
    lastb = jnp.concatenate(lastb, axis=0)
    yield

    q = h[:, 0:GLA_WIDTH] * (GLA_DK ** -0.5)
    k = h[:, GLA_WIDTH:2 * GLA_WIDTH]
    qe = (q * jnp.exp(bcum - refb)).astype(BF16)
    ke = (k * jnp.exp(refb - bcum)).astype(BF16)
    qd = (q * jnp.exp(bcum)).astype(BF16)
    kd = (k * jnp.exp(lastb - bcum)).astype(BF16)
    vb = h[:, 2 * GLA_WIDTH:3 * GLA_WIDTH].astype(BF16)
    yield

    bd = _block_diag_mask(GLA_WIDTH, GLA_DK)
    bd16 = bdm_ref[...] > 0
    zero16 = jnp.zeros((GLA_WIDTH, GLA_WIDTH), BF16)
    ri4 = lax.broadcasted_iota(jnp.int32, (CHUNK, GLA_WIDTH), 0)
    ci4 = lax.broadcasted_iota(jnp.int32, (CHUNK, GLA_WIDTH), 1) % CHUNK
    causal4 = ci4 <= ri4

    state = state_ref[...]
    for c in range(t // CHUNK):
        rows = slice(c * CHUNK, (c + 1) * CHUNK)
        kstack = jnp.where(bd16, jnp.concatenate([ke[rows]] * GLA_HEADS, axis=0), zero16)
        vstack = jnp.where(bd16, jnp.concatenate([vb[rows]] * GLA_HEADS, axis=0), zero16)
        attn = _dot_nt(qe[rows], kstack)
        attn = jnp.where(causal4, attn, 0.0).astype(BF16)
        o_ref[rows, :] = _dot(attn, vstack) + _dot_nt(qd[rows], state.astype(BF16))
        ut = _dot_tn(vb[rows], kd[rows])
        decay = jnp.exp(lastb[c * CHUNK:c * CHUNK + 1, :])
        state = state * decay + jnp.where(bd, ut, 0.0)
        if c % 2 == 1:
            yield
    state_ref[...] = state

    o = o_ref[...]
    group_mean = jnp.where(bd16, jnp.full_like(zero16, 1.0 / GLA_DK), zero16)
    ms = _dot((o * o).astype(BF16), group_mean)
    g = h[:, 3 * GLA_WIDTH:4 * GLA_WIDTH]
    y_ref[...] = (o * lax.rsqrt(ms + RMS_EPS) * ng_ref[...] * (g * jax.nn.sigmoid(g))).astype(BF16)


def _lru_rows(get_h, cw_ref, cb_ref, wa_ref, ba_ref, wi_ref, bi_ref, lam_ref,
              xtail, hcarry, xbuf, a_scr, u_scr, h_scr, hin_scr, y_ref):
    h = get_h()
    t = h.shape[0]
    n_groups = t // SUBLANES
    xbuf[0:SUBLANES, :] = xtail[...]
    xbuf[SUBLANES:SUBLANES + t, :] = h[:, 0:LRU_WIDTH]
    xc = cb_ref[...] + jnp.zeros((t, LRU_WIDTH), F32)
    for w in range(CONV_WIDTH):
        off = SUBLANES - (CONV_WIDTH - 1) + w
        xc = xc + cw_ref[w:w + 1, :] * xbuf[off:off + t, :]
    xtail[...] = xbuf[t:t + SUBLANES, :]
    yield

    xcb = xc.astype(BF16)
    r = jax.nn.sigmoid(_dot(xcb, wa_ref[...]) + ba_ref[...])
    ig = jax.nn.sigmoid(_dot(xcb, wi_ref[...]) + bi_ref[...])
    log_a = (-LRU_C) * r * _softplus(-lam_ref[...])
    a = jnp.exp(log_a)
    th = jnp.tanh(log_a)
    u = jnp.sqrt(-2.0 * th / (1.0 - th)) * (ig * xc)
    yield

    row_in_group = lax.broadcasted_iota(jnp.int32, (t, LRU_WIDTH), 0) % SUBLANES
    d = 1
    while d < SUBLANES:
        u = u + a * _shift_rows(u, d, 0.0, row_in_group)
        a = a * _shift_rows(a, d, 1.0, row_in_group)
        d *= 2
    lane_halves = [slice(v * LANES, (v + 1) * LANES) for v in range(LRU_WIDTH // LANES)]
    group_last = pl.ds(SUBLANES - 1, n_groups, stride=SUBLANES)
    for v, lanes in enumerate(lane_halves):
        a_scr[v] = a[:, lanes]
        u_scr[v] = u[:, lanes]
    yield
    a_g = jnp.concatenate([a_scr[v, group_last, :] for v in range(len(lane_halves))], axis=1)
    u_g = jnp.concatenate([u_scr[v, group_last, :] for v in range(len(lane_halves))], axis=1)
    group_idx = lax.broadcasted_iota(jnp.int32, (n_groups, LRU_WIDTH), 0)
    d = 1
    while d < n_groups:
        u_g = u_g + a_g * _shift_rows(u_g, d, 0.0, group_idx)
        a_g = a_g * _shift_rows(a_g, d, 1.0, group_idx)
        d *= 2
    h_prev = hcarry[0:1, :]
    h_end = u_g + a_g * h_prev
    hin_scr[...] = _shift_rows(h_end, 1, h_prev, group_idx)
    hcarry[...] = jnp.broadcast_to(h_end[n_groups - 1:n_groups, :], hcarry.shape)
    yield
    for g in range(n_groups):
        rows = slice(g * SUBLANES, (g + 1) * SUBLANES)
        for v, lanes in enumerate(lane_halves):
            h_scr[rows, lanes] = u_scr[v, rows, :] + a_scr[v, rows, :] * hin_scr[g:g + 1, lanes]
    y_ref[...] = (h_scr[...] * jax.nn.gelu(h[:, LRU_WIDTH:2 * LRU_WIDTH])).astype(BF16)


def _round_robin(*gens):
    pending = list(gens)
    while pending:
        alive = []
        for g in pending:
            if next(g, StopIteration) is not StopIteration:
                alive.append(g)
                yield
        pending = alive


def _mixer_in_kernel(x_ref, w_ref, wup_ref, bup_ref, ng_ref, bdm_ref,
                     cw_ref, cb_ref, wa_ref, ba_ref, wi_ref, bi_ref, lam_ref,
                     att_ref, yg_ref, yl_ref,
                     state_scr, xtail, hcarry, o_scr, xbuf, a_scr, u_scr, h_scr, hin_scr):
    @pl.when(pl.program_id(1) == 0)
    def _():
        state_scr[...] = jnp.zeros_like(state_scr)
        xtail[...] = jnp.zeros_like(xtail)
        hcarry[...] = jnp.zeros_like(hcarry)

    xb = x_ref[...].astype(BF16)
    out = {}

    def project(name, off, width):
        cols = []
        for c in range(0, width, PROJ_PIECE):
            n_cols = min(PROJ_PIECE, width - c)
            cols.append(_dot(xb, w_ref[:, off + c:off + c + n_cols]))
            if c + n_cols == width:
                out[name] = jnp.concatenate(cols, axis=1) if len(cols) > 1 else cols[0]
            yield

    def project_attention():
        for c in range(0, ATT_COLS, PROJ_PIECE):
            att_ref[:, c:c + PROJ_PIECE] = _dot(
                xb, w_ref[:, ATT_OFF + c:ATT_OFF + c + PROJ_PIECE]).astype(BF16)
            yield

    lru = _lru_rows(lambda: out["lru"], cw_ref, cb_ref, wa_ref, ba_ref, wi_ref, bi_ref, lam_ref,
                    xtail, hcarry, xbuf, a_scr, u_scr, h_scr, hin_scr, yl_ref)
    gla = _gla_rows(lambda: (out["gla"], out["glr"]), wup_ref, bup_ref, ng_ref, bdm_ref,
                    o_scr, state_scr, yg_ref)
    stages = (project("lru", LRU_OFF, LRU_COLS),
              _round_robin(itertools.chain(project("glr", GLR_OFF, GLR_COLS),
                                           project("gla", 0, GLA_COLS)), lru),
              _round_robin(project_attention(), gla))
    for _ in itertools.chain(*stages):
        pass


def _mixer_in(layer, x2, w_perm, wup_pad, bup, ng_row, conv_w, conv_b, wa_bd, b_a, wi_bd, b_i, lam,
              batch):
    n = x2.shape[0]
    nt = n // batch // MIX_TILE
    d = np.arange(GLA_WIDTH) // GLA_DK
    bdm = jnp.asarray((d[:, None] == d[None, :]).astype(np.float32), BF16)
    row = lambda c: pl.BlockSpec((MIX_TILE, c), lambda b, i: (b * nt + i, 0))
    full = lambda r, c: pl.BlockSpec((r, c), lambda b, i: (0, 0))
    of_layer = lambda r, c: pl.BlockSpec((None, r, c), lambda b, i: (layer, 0, 0))
    tile = lambda r, c: pltpu.VMEM((r, c), F32)
    lane_halves = pltpu.VMEM((LRU_WIDTH // LANES, MIX_TILE, LANES), F32)
    return pl.pallas_call(
        _mixer_in_kernel,
        grid=(batch, nt),
        in_specs=[row(D_MODEL),
                  pl.BlockSpec((None, D_MODEL, D_IN_PAD), lambda b, i: (layer, 0, 0),
                               pipeline_mode=pl.Buffered(1)),
                  of_layer(GLR_COLS, GLA_WIDTH), of_layer(1, GLA_WIDTH), of_layer(1, GLA_WIDTH),
                  full(GLA_WIDTH, GLA_WIDTH),
                  of_layer(CONV_WIDTH, LRU_WIDTH), of_layer(1, LRU_WIDTH),
                  of_layer(LRU_WIDTH, LRU_WIDTH), of_layer(1, LRU_WIDTH),
                  of_layer(LRU_WIDTH, LRU_WIDTH), of_layer(1, LRU_WIDTH), of_layer(1, LRU_WIDTH)],
        out_specs=[row(ATT_COLS), row(GLA_WIDTH), row(LRU_WIDTH)],
        out_shape=[jax.ShapeDtypeStruct((n, ATT_COLS), BF16),
                   jax.ShapeDtypeStruct((n, GLA_WIDTH), BF16),
                   jax.ShapeDtypeStruct((n, LRU_WIDTH), BF16)],
        scratch_shapes=[pltpu.VMEM((GLA_WIDTH, GLA_WIDTH), F32),
                        pltpu.VMEM((SUBLANES, LRU_WIDTH), F32),
                        pltpu.VMEM((SUBLANES, LRU_WIDTH), F32),
                        tile(MIX_TILE, GLA_WIDTH),
                        tile(MIX_TILE + SUBLANES, LRU_WIDTH),
                        lane_halves, lane_halves,
                        tile(MIX_TILE, LRU_WIDTH),
                        tile(MIX_TILE // SUBLANES, LRU_WIDTH)],
        compiler_params=pltpu.CompilerParams(
            dimension_semantics=("arbitrary", "arbitrary"), vmem_limit_bytes=VMEM_LIMIT_BYTES),
        name="mixer_in",
    )(x2, w_perm, wup_pad, bup, ng_row, bdm, conv_w, conv_b, wa_bd, b_a, wi_bd, b_i, lam)


def _att_kernel(q_ref, kw_ref, vw_ref, diag_ref, hm_ref, y_ref, bias_ref):
    lane_head0 = lax.broadcasted_iota(jnp.int32, (ATT_SUB, LANES), 1) < ATT_DH

    @pl.when((pl.program_id(1) == 0) & (pl.program_id(2) == 0))
    def _():
        r = lax.broadcasted_iota(jnp.int32, (ATT_SUB, ATT_WIN), 0)
        w = lax.broadcasted_iota(jnp.int32, (ATT_SUB, ATT_WIN), 1)
        kk = w - (r // CHUNK) * CHUNK
        in_band = (kk >= 0) & (kk < (ATT_LEFT_CHUNKS + 1) * CHUNK)
        for hh in range(ATT_PAIRS * HEADS_PER_PAIR):
            rows = jnp.broadcast_to(diag_ref[hh:hh + 1, :], (ATT_SUB, ATT_DIAG))
            skew = pltpu.roll(rows, ATT_DIAG - (ATT_SUB - 1), axis=1, stride=1, stride_axis=0)
            bias_ref[hh] = jnp.where(in_band, skew[:, 0:ATT_WIN], NEG_INF)

    def body(first):
        n_keys = ROW_TILE if first else 2 * ROW_TILE
        head0 = hm_ref[0:n_keys, :] > 0
        vh, qh = [], []
        for pair in range(ATT_PAIRS):
            lanes = slice(pair * LANES, (pair + 1) * LANES)
            v = vw_ref[0:n_keys, lanes]
            one16 = jnp.ones_like(v)
            vh.append((jnp.where(head0, v, one16), jnp.where(head0, one16, v)))
            q = q_ref[:, lanes]
            zero16 = jnp.zeros_like(q)
            qh.append((jnp.where(head0[0:ROW_TILE], q, zero16),
                       jnp.where(head0[0:ROW_TILE], zero16, q)))

        def window(j):
            if first:
                return slice(0, (j + 1) * ATT_SUB)
            return slice(j * ATT_SUB, j * ATT_SUB + ATT_WIN)

        def scores(pair, j, hh):
            win = window(j)
            n_win = win.stop - win.start
            rows = slice(j * ATT_SUB, (j + 1) * ATT_SUB)
            return (_dot_nt(qh[pair][hh][rows], kw_ref[win, pair * LANES:(pair + 1) * LANES])
                    + bias_ref[pair * HEADS_PER_PAIR + hh, :, ATT_WIN - n_win:ATT_WIN])

        units = [(pair, j, hh) for pair in range(ATT_PAIRS)
                 for j in range(ROW_TILE // ATT_SUB) for hh in range(HEADS_PER_PAIR)]
        n_units = len(units)
        s, p, outs = {}, {}, []
        for step in range(n_units + 2):
            if step < n_units:
                s[step] = scores(*units[step])
            if step >= 2:
                pair, j, hh = units[step - 2]
                of = _dot(p.pop(step - 2), vh[pair][hh][window(j)])
                outs.append(of / pltpu.roll(of, ATT_DH, axis=1))
                if hh == HEADS_PER_PAIR - 1:
                    rows = slice(j * ATT_SUB, (j + 1) * ATT_SUB)
                    y_ref[rows, pair * LANES:(pair + 1) * LANES] = jnp.where(
                        lane_head0, outs[0], outs[1]).astype(BF16)
                    outs = []
            if 1 <= step <= n_units:
                su = s.pop(step - 1)
                m = jnp.max(su, axis=-1, keepdims=True)
                p[step - 1] = jnp.exp2(su - m).astype(BF16)

    @pl.when(pl.program_id(2) == 0)
    def _():
        body(True)

    @pl.when(pl.program_id(2) > 0)
    def _():
        body(False)


def _attention(att_h, diag, batch):
    n = att_h.shape[0]
    seq = n // batch
    att3 = att_h.reshape(batch, seq, ATT_COLS)
    groups = ATT_WIDTH // (ATT_PAIRS * LANES)
    width = ATT_PAIRS * LANES
    head0 = jnp.asarray(
        np.broadcast_to(np.arange(LANES) < ATT_DH, (2 * ROW_TILE, LANES)).astype(np.float32), BF16)
    window = lambda off: pl.BlockSpec(
        (None, pl.Element(2 * ROW_TILE), pl.Element(width)),
        lambda g, b, i: (b, jnp.maximum(i - 1, 0) * ROW_TILE, (off + g * ATT_PAIRS) * LANES))
    y = pl.pallas_call(
        _att_kernel,
        grid=(groups, batch, seq // ROW_TILE),
        in_specs=[pl.BlockSpec((None, ROW_TILE, width), lambda g, b, i: (b, i, g)),
                  window(ATT_WIDTH // LANES), window(2 * ATT_WIDTH // LANES),
                  pl.BlockSpec((None, ATT_PAIRS * HEADS_PER_PAIR, ATT_DIAG),
                               lambda g, b, i: (g, 0, 0)),
                  pl.BlockSpec((2 * ROW_TILE, LANES), lambda g, b, i: (0, 0))],
        out_specs=pl.BlockSpec((None, ROW_TILE, width), lambda g, b, i: (b, i, g)),
        out_shape=jax.ShapeDtypeStruct((batch, seq, ATT_WIDTH), BF16),
        scratch_shapes=[pltpu.VMEM((ATT_PAIRS * HEADS_PER_PAIR, ATT_SUB, ATT_WIN), F32)],
        compiler_params=pltpu.CompilerParams(
            dimension_semantics=("arbitrary", "arbitrary", "arbitrary"),
            vmem_limit_bytes=VMEM_LIMIT_BYTES),
        name="band_attention",
    )(att3, att3, att3, diag, head0)
    return y.reshape(n, ATT_WIDTH)


def _attention_diag(rel_bias):
    n_diag = ATT_SUB + ATT_WIN - 1
    rel = ATT_LEFT_CHUNKS * CHUNK + (ATT_SUB - 1) - np.arange(n_diag)
    idx = np.clip(rel, -REL_CLIP, REL_CLIP) + REL_CLIP
    diag = jnp.pad(rel_bias.astype(F32)[:, idx] * LOG2_E, ((0, 0), (0, ATT_DIAG - n_diag)))
    return diag.reshape(-1, ATT_PAIRS * HEADS_PER_PAIR, ATT_DIAG)


def _post_kernel(yg_ref, ya_ref, yl_ref, x_ref, p_ref, wo_ref, g1_ref, b1_ref,
                 wg_ref, wu_ref, wd_ref, wpg_ref, wpp_ref, g2_ref, b2_ref, o_ref, act_ref):
    n_parts = POST_TILE // POST_ROWS
    rows = [slice(k * POST_ROWS, (k + 1) * POST_ROWS) for k in range(n_parts)]
    n_ff = D_FF // FF_TILE
    mix, x1, xb, ffn, ple = ({} for _ in range(5))

    def out_projection(k):
        mix[k] = (_dot(yg_ref[rows[k], :], wo_ref[0:GLA_WIDTH, :])
                  + _dot(ya_ref[rows[k], :], wo_ref[GLA_WIDTH:GLA_WIDTH + ATT_WIDTH, :])
                  + _dot(yl_ref[rows[k], :], wo_ref[GLA_WIDTH + ATT_WIDTH:D_MODEL, :]))

    def norm1(k):
        x1[k] = _layer_norm(DN_ALPHA * x_ref[rows[k], :] + mix.pop(k), g1_ref[...], b1_ref[...])
        xb[k] = x1[k].astype(BF16)

    def ffn_up(k, c):
        cols = slice(c * FF_TILE, (c + 1) * FF_TILE)
        gate = _dot(xb[k], wg_ref[:, cols])
        up = _dot(xb[k], wu_ref[:, cols])
        act_ref[k, :, cols] = (gate * jax.nn.sigmoid(gate) * up).astype(BF16)

    def ffn_down(k):
        ffn[k] = _dot(act_ref[k], wd_ref[...])
        ple[k] = (jax.nn.sigmoid(_dot(xb.pop(k), wpg_ref[...]))
                  * _dot(p_ref[rows[k], :].astype(BF16), wpp_ref[...]))

    def norm2(k):
        o_ref[rows[k], :] = _layer_norm(DN_ALPHA * x1.pop(k) + ffn.pop(k) + ple.pop(k),
                                        g2_ref[...], b2_ref[...])

    for k in range(n_parts):
        out_projection(k)
    norm1(0)
    for k in range(n_parts):
        for c in range(n_ff):
            ffn_up(k, c)
            if c == n_ff // 2 and k + 1 < n_parts:
                norm1(k + 1)
    for k in range(n_parts):
        ffn_down(k)
        if k > 0:
            norm2(k - 1)
    norm2(n_parts - 1)


def _post(layer, y_gla, y_att, y_lru, x2, p3, wo, g1, b1, wg, wu, wd, wpg, wpp, g2, b2):
    n = x2.shape[0]
    row = lambda c: pl.BlockSpec((POST_TILE, c), lambda i: (i, 0))
    full = lambda r, c: pl.BlockSpec((None, r, c), lambda i: (layer, 0, 0),
                                     pipeline_mode=pl.Buffered(1))
    return pl.pallas_call(
        _post_kernel,
        grid=(n // POST_TILE,),
        in_specs=[row(GLA_WIDTH), row(ATT_WIDTH), row(LRU_WIDTH), row(D_MODEL),
                  pl.BlockSpec((None, POST_TILE, PLE_DIM), lambda i: (layer, i, 0)),
                  full(D_MODEL, D_MODEL), full(1, D_MODEL), full(1, D_MODEL),
                  full(D_MODEL, D_FF), full(D_MODEL, D_FF), full(D_FF, D_MODEL),
                  full(D_MODEL, D_MODEL), full(PLE_DIM, D_MODEL),
                  full(1, D_MODEL), full(1, D_MODEL)],
        out_specs=row(D_MODEL),
        out_shape=jax.ShapeDtypeStruct((n, D_MODEL), F32),
        scratch_shapes=[pltpu.VMEM((POST_TILE // POST_ROWS, POST_ROWS, D_FF), BF16)],
        compiler_params=pltpu.CompilerParams(
            dimension_semantics=("arbitrary",), vmem_limit_bytes=VMEM_LIMIT_BYTES),
        name="out_proj_ffn",
    )(y_gla, y_att, y_lru, x2, p3, wo, g1, b1, wg, wu, wd, wpg, wpp, g2, b2)


def _block_diag(w):
    n_layers, g, c, d = w.shape
    eye = jnp.eye(g, dtype=w.dtype)
    return (eye[None, :, None, :, None] * w[:, :, :, None, :]).reshape(n_layers, g * c, g * d)


def _permute_w_in(w):
    glr_lo = GLA_COLS
    glr_hi = GLA_COLS + GLA_GATE_RANK
    aq_hi = glr_hi + ATT_WIDTH
    pad = jnp.zeros(w.shape[:2] + (GLR_COLS - GLA_GATE_RANK,), w.dtype)
    return jnp.concatenate(
        [w[..., :glr_lo], w[..., glr_hi:aq_hi] * (ATT_DH ** -0.5 * LOG2_E), w[..., aq_hi:],
         w[..., glr_lo:glr_hi], pad], axis=-1).astype(BF16)


def kernel(x, p, w_in, gla_w_lr_up, gla_b_lr_up, gla_norm_g, rel_bias, lru_conv_w, lru_conv_b,
           lru_w_a, lru_b_a, lru_w_i, lru_b_i, lru_lambda, w_out, ln1_g, ln1_b, w_ffn_gate,
           w_ffn_up, w_ffn_down, w_ple_gate, w_ple_proj, ln2_g, ln2_b):
    batch, seq, _ = x.shape
    n = batch * seq
    assert seq % ROW_TILE == 0
    x2 = x.reshape(n, D_MODEL)
    att_diag = _attention_diag(rel_bias)
    rows = lambda a: a.reshape(DEPTH, 1, -1).astype(F32)
    bf16 = lambda a: a.astype(BF16)

    mixer_params = (
        _permute_w_in(w_in),
        bf16(jnp.pad(gla_w_lr_up, ((0, 0), (0, GLR_COLS - GLA_GATE_RANK), (0, 0)))),
        rows(gla_b_lr_up), rows(jnp.tile(gla_norm_g, (1, GLA_HEADS))),
        lru_conv_w.astype(F32), rows(lru_conv_b),
        bf16(_block_diag(lru_w_a)), rows(lru_b_a),
        bf16(_block_diag(lru_w_i)), rows(lru_b_i), rows(lru_lambda))
    post_params = (
        bf16(w_out), rows(ln1_g), rows(ln1_b), bf16(w_ffn_gate), bf16(w_ffn_up), bf16(w_ffn_down),
        bf16(w_ple_gate), bf16(w_ple_proj), rows(ln2_g), rows(ln2_b))
    p3 = p.reshape(DEPTH, n, PLE_DIM)

    for layer in range(DEPTH):
        att_h, y_gla, y_lru = _mixer_in(layer, x2, *mixer_params, batch)
        y_att = _attention(att_h, att_diag, batch)
        x2 = _post(layer, y_gla, y_att, y_lru, x2, p3, *post_params)
    return x2.reshape(batch, seq, D_MODEL)
```

```python
import itertools
import math

import numpy as np
import jax
import jax.numpy as jnp
from jax import lax
from jax.experimental import pallas as pl
from jax.experimental.pallas import tpu as pltpu

F32 = jnp.float32
BF16 = jnp.bfloat16

D_MODEL = 1024
CHUNK = 64
GLA_HEADS = 4
GLA_DK = 64
GLA_WIDTH = 256
GLA_GATE_RANK = 16
GLA_GATE_TAU = 16.0
ATT_HEADS = 8
ATT_DH = 64
ATT_WIDTH = 512
ATT_LEFT_CHUNKS = 8
REL_CLIP = 256
LRU_WIDTH = 256
LRU_BLOCKS = 4
LRU_C = 8.0
CONV_WIDTH = 4
D_FF = 2816
PLE_DIM = 256
DEPTH = 2
DN_ALPHA = (2 * DEPTH) ** 0.25
LN_EPS = 1e-5
RMS_EPS = 1e-6
NEG_INF = -1e30
LOG2_E = math.log2(math.e)

LANES = 128
SUBLANES = 8
VMEM_LIMIT_BYTES = 56 * 1024 * 1024

GLA_COLS = 4 * GLA_WIDTH
ATT_COLS = 3 * ATT_WIDTH
LRU_COLS = 2 * LRU_WIDTH
GLR_COLS = LANES
ATT_OFF = GLA_COLS
LRU_OFF = ATT_OFF + ATT_COLS
GLR_OFF = LRU_OFF + LRU_COLS
D_IN_PAD = GLR_OFF + GLR_COLS

ROW_TILE = 512
MIX_TILE = 512
ATT_SUB = 256
ATT_WIN = ATT_SUB + ATT_LEFT_CHUNKS * CHUNK
ATT_DIAG = ATT_SUB + ATT_WIN
HEADS_PER_PAIR = LANES // ATT_DH
ATT_PAIRS = 4
FF_TILE = 256
PROJ_PIECE = 512
POST_TILE = 512
POST_ROWS = 256


def _dot(a, b):
    return jnp.dot(a, b, preferred_element_type=F32)


def _dot_nt(a, b):
    return lax.dot_general(a, b, (((1,), (1,)), ((), ())), preferred_element_type=F32)


def _dot_tn(a, b):
    return lax.dot_general(a, b, (((0,), (0,)), ((), ())), preferred_element_type=F32)


def _block_diag_mask(n, blk):
    r = lax.broadcasted_iota(jnp.int32, (n, n), 0) // blk
    c = lax.broadcasted_iota(jnp.int32, (n, n), 1) // blk
    return r == c


def _layer_norm(z, g, b):
    mu = jnp.mean(z, axis=-1, keepdims=True)
    zc = z - mu
    var = jnp.mean(zc * zc, axis=-1, keepdims=True)
    return zc * lax.rsqrt(var + LN_EPS) * g + b


def _softplus(x):
    return jnp.maximum(x, 0.0) + jnp.log1p(jnp.exp(-jnp.abs(x)))


def _shift_rows(x, d, fill, row_idx):
    return jnp.where(row_idx >= d, pltpu.roll(x, d, axis=0), fill)


def _gla_rows(get_h, wup_ref, bup_ref, ng_ref, bdm_ref, o_ref, state_ref, y_ref):
    h, lr = get_h()
    t = h.shape[0]
    z = _dot(lr.astype(BF16), wup_ref[...]) + bup_ref[...]
    la = -_softplus(-z) * (1.0 / GLA_GATE_TAU)
    la_hi = la.astype(BF16)
    la_lo = (la - la_hi.astype(F32)).astype(BF16)
    ri = lax.broadcasted_iota(jnp.int32, (CHUNK, CHUNK), 0)
    ci = lax.broadcasted_iota(jnp.int32, (CHUNK, CHUNK), 1)
    tri = jnp.where(ci <= ri, 1.0, 0.0).astype(BF16)
    bcum, refb, lastb = [], [], []
    for c in range(t // CHUNK):
        rows = slice(c * CHUNK, (c + 1) * CHUNK)
        bc = _dot(tri, la_hi[rows]) + _dot(tri, la_lo[rows])
        bcum.append(bc)
        refb.append(jnp.broadcast_to(bc[CHUNK // 2:CHUNK // 2 + 1], bc.shape))
        lastb.append(jnp.broadcast_to(bc[CHUNK - 1:CHUNK], bc.shape))
    bcum = jnp.concatenate(bcum, axis=0)
    refb = jnp.concatenate(refb, axis=0)
    lastb = jnp.concatenate(lastb, axis=0)
    yield

    q = h[:, 0:GLA_WIDTH] * (GLA_DK ** -0.5)
    k = h[:, GLA_WIDTH:2 * GLA_WIDTH]
    qe = (q * jnp.exp(bcum - refb)).astype(BF16)
    ke = (k * jnp.exp(refb - bcum)).astype(BF16)
    qd = (q * jnp.exp(bcum)).astype(BF16)
    kd = (k * jnp.exp(lastb - bcum)).astype(BF16)
    vb = h[:, 2 * GLA_WIDTH:3 * GLA_WIDTH].astype(BF16)
    yield

    bd = _block_diag_mask(GLA_WIDTH, GLA_DK)
    bd16 = bdm_ref[...] > 0
    zero16 = jnp.zeros((GLA_WIDTH, GLA_WIDTH), BF16)
    ri4 = lax.broadcasted_iota(jnp.int32, (CHUNK, GLA_WIDTH), 0)
    ci4 = lax.broadcasted_iota(jnp.int32, (CHUNK, GLA_WIDTH), 1) % CHUNK
    causal4 = ci4 <= ri4

    state = state_ref[...]
    for c in range(t // CHUNK):
        rows = slice(c * CHUNK, (c + 1) * CHUNK)
        kstack = jnp.where(bd16, jnp.concatenate([ke[rows]] * GLA_HEADS, axis=0), zero16)
        vstack = jnp.where(bd16, jnp.concatenate([vb[rows]] * GLA_HEADS, axis=0), zero16)
        attn = _dot_nt(qe[rows], kstack)
        attn = jnp.where(causal4, attn, 0.0).astype(BF16)
        o_ref[rows, :] = _dot(attn, vstack) + _dot_nt(qd[rows], state.astype(BF16))
        ut = _dot_tn(vb[rows], kd[rows])
        decay = jnp.exp(lastb[c * CHUNK:c * CHUNK + 1, :])
        state = state * decay + jnp.where(bd, ut, 0.0)
        if c % 2 == 1:
            yield
    state_ref[...] = state

    o = o_ref[...]
    group_mean = jnp.where(bd16, jnp.full_like(zero16, 1.0 / GLA_DK), zero16)
    ms = _dot((o * o).astype(BF16), group_mean)
    g = h[:, 3 * GLA_WIDTH:4 * GLA_WIDTH]
    y_ref[...] = (o * lax.rsqrt(ms + RMS_EPS) * ng_ref[...] * (g * jax.nn.sigmoid(g))).astype(BF16)


def _lru_rows(get_h, cw_ref, cb_ref, wa_ref, ba_ref, wi_ref, bi_ref, lam_ref,
              xtail, hcarry, xbuf, a_scr, u_scr, h_scr, hin_scr, y_ref):
    h = get_h()
    t = h.shape[0]
    n_groups = t // SUBLANES
    xbuf[0:SUBLANES, :] = xtail[...]
    xbuf[SUBLANES:SUBLANES + t, :] = h[:, 0:LRU_WIDTH]
    xc = cb_ref[...] + jnp.zeros((t, LRU_WIDTH), F32)
    for w in range(CONV_WIDTH):
        off = SUBLANES - (CONV_WIDTH - 1) + w
        xc = xc + cw_ref[w:w + 1, :] * xbuf[off:off + t, :]
    xtail[...] = xbuf[t:t + SUBLANES, :]
    yield

    xcb = xc.astype(BF16)
    r = jax.nn.sigmoid(_dot(xcb, wa_ref[...]) + ba_ref[...])
    ig = jax.nn.sigmoid(_dot(xcb, wi_ref[...]) + bi_ref[...])
    log_a = (-LRU_C) * r * _softplus(-lam_ref[...])
    a = jnp.exp(log_a)
    th = jnp.tanh(log_a)
    u = jnp.sqrt(-2.0 * th / (1.0 - th)) * (ig * xc)
    yield

    row_in_group = lax.broadcasted_iota(jnp.int32, (t, LRU_WIDTH), 0) % SUBLANES
    d = 1
    while d < SUBLANES:
        u = u + a * _shift_rows(u, d, 0.0, row_in_group)
        a = a * _shift_rows(a, d, 1.0, row_in_group)
        d *= 2
    lane_halves = [slice(v * LANES, (v + 1) * LANES) for v in range(LRU_WIDTH // LANES)]
    group_last = pl.ds(SUBLANES - 1, n_groups, stride=SUBLANES)
    for v, lanes in enumerate(lane_halves):
        a_scr[v] = a[:, lanes]
        u_scr[v] = u[:, lanes]
    yield
    a_g = jnp.concatenate([a_scr[v, group_last, :] for v in range(len(lane_halves))], axis=1)
    u_g = jnp.concatenate([u_scr[v, group_last, :] for v in range(len(lane_halves))], axis=1)
    group_idx = lax.broadcasted_iota(jnp.int32, (n_groups, LRU_WIDTH), 0)
    d = 1
    while d < n_groups:
        u_g = u_g + a_g * _shift_rows(u_g, d, 0.0, group_idx)
        a_g = a_g * _shift_rows(a_g, d, 1.0, group_idx)
        d *= 2
    h_prev = hcarry[0:1, :]
    h_end = u_g + a_g * h_prev
    hin_scr[...] = _shift_rows(h_end, 1, h_prev, group_idx)
    hcarry[...] = jnp.broadcast_to(h_end[n_groups - 1:n_groups, :], hcarry.shape)
    yield
    for g in range(n_groups):
        rows = slice(g * SUBLANES, (g + 1) * SUBLANES)
        for v, lanes in enumerate(lane_halves):
            h_scr[rows, lanes] = u_scr[v, rows, :] + a_scr[v, rows, :] * hin_scr[g:g + 1, lanes]
    y_ref[...] = (h_scr[...] * jax.nn.gelu(h[:, LRU_WIDTH:2 * LRU_WIDTH])).astype(BF16)


def _round_robin(*gens):
    pending = list(gens)
    while pending:
        alive = []
        for g in pending:
            if next(g, StopIteration) is not StopIteration:
                alive.append(g)
                yield
        pending = alive


def _mixer_in_kernel(x_ref, w_ref, wup_ref, bup_ref, ng_ref, bdm_ref,
                     cw_ref, cb_ref, wa_ref, ba_ref, wi_ref, bi_ref, lam_ref,
                     att_ref, yg_ref, yl_ref,
                     state_scr, xtail, hcarry, o_scr, xbuf, a_scr, u_scr, h_scr, hin_scr):
    @pl.when(pl.program_id(1) == 0)
    def _():
        state_scr[...] = jnp.zeros_like(state_scr)
        xtail[...] = jnp.zeros_like(xtail)
        hcarry[...] = jnp.zeros_like(hcarry)

    xb = x_ref[...].astype(BF16)
    out = {}

    def project(name, off, width):
        cols = []
        for c in range(0, width, PROJ_PIECE):
            n_cols = min(PROJ_PIECE, width - c)
            cols.append(_dot_nt(xb, w_ref[off + c:off + c + n_cols, :]))
            if c + n_cols == width:
                out[name] = jnp.concatenate(cols, axis=1) if len(cols) > 1 else cols[0]
            yield

    def project_attention():
        for c in range(0, ATT_COLS, PROJ_PIECE):
            att_ref[:, c:c + PROJ_PIECE] = _dot_nt(
                xb, w_ref[ATT_OFF + c:ATT_OFF + c + PROJ_PIECE, :]).astype(BF16)
            yield

    lru = _lru_rows(lambda: out["lru"], cw_ref, cb_ref, wa_ref, ba_ref, wi_ref, bi_ref, lam_ref,
                    xtail, hcarry, xbuf, a_scr, u_scr, h_scr, hin_scr, yl_ref)
    gla = _gla_rows(lambda: (out["gla"], out["glr"]), wup_ref, bup_ref, ng_ref, bdm_ref,
                    o_scr, state_scr, yg_ref)
    stages = (project("lru", LRU_OFF, LRU_COLS),
              _round_robin(itertools.chain(project("glr", GLR_OFF, GLR_COLS),
                                           project("gla", 0, GLA_COLS)), lru),
              _round_robin(project_attention(), gla))
    for _ in itertools.chain(*stages):
        pass


def _mixer_in(layer, x2, w_perm, wup_pad, bup, ng_row, conv_w, conv_b, wa_bd, b_a, wi_bd, b_i, lam,
              batch):
    n = x2.shape[0]
    nt = n // batch // MIX_TILE
    d = np.arange(GLA_WIDTH) // GLA_DK
    bdm = jnp.asarray((d[:, None] == d[None, :]).astype(np.float32), BF16)
    row = lambda c: pl.BlockSpec((MIX_TILE, c), lambda b, i: (b * nt + i, 0))
    full = lambda r, c: pl.BlockSpec((r, c), lambda b, i: (0, 0))
    of_layer = lambda r, c: pl.BlockSpec((None, r, c), lambda b, i: (layer, 0, 0))
    tile = lambda r, c: pltpu.VMEM((r, c), F32)
    lane_halves = pltpu.VMEM((LRU_WIDTH // LANES, MIX_TILE, LANES), F32)
    return pl.pallas_call(
        _mixer_in_kernel,
        grid=(batch, nt),
        in_specs=[row(D_MODEL),
                  pl.BlockSpec((None, D_IN_PAD, D_MODEL), lambda b, i: (layer, 0, 0),
                               pipeline_mode=pl.Buffered(1)),
                  of_layer(GLR_COLS, GLA_WIDTH), of_layer(1, GLA_WIDTH), of_layer(1, GLA_WIDTH),
                  full(GLA_WIDTH, GLA_WIDTH),
                  of_layer(CONV_WIDTH, LRU_WIDTH), of_layer(1, LRU_WIDTH),
                  of_layer(LRU_WIDTH, LRU_WIDTH), of_layer(1, LRU_WIDTH),
                  of_layer(LRU_WIDTH, LRU_WIDTH), of_layer(1, LRU_WIDTH), of_layer(1, LRU_WIDTH)],
        out_specs=[row(ATT_COLS), row(GLA_WIDTH), row(LRU_WIDTH)],
        out_shape=[jax.ShapeDtypeStruct((n, ATT_COLS), BF16),
                   jax.ShapeDtypeStruct((n, GLA_WIDTH), BF16),
                   jax.ShapeDtypeStruct((n, LRU_WIDTH), BF16)],
        scratch_shapes=[pltpu.VMEM((GLA_WIDTH, GLA_WIDTH), F32),
                        pltpu.VMEM((SUBLANES, LRU_WIDTH), F32),
                        pltpu.VMEM((SUBLANES, LRU_WIDTH), F32),
                        tile(MIX_TILE, GLA_WIDTH),
                        tile(MIX_TILE + SUBLANES, LRU_WIDTH),
                        lane_halves, lane_halves,
                        tile(MIX_TILE, LRU_WIDTH),
                        tile(MIX_TILE // SUBLANES, LRU_WIDTH)],
        compiler_params=pltpu.CompilerParams(
            dimension_semantics=("arbitrary", "arbitrary"), vmem_limit_bytes=VMEM_LIMIT_BYTES),
        name="mixer_in",
    )(x2, w_perm, wup_pad, bup, ng_row, bdm, conv_w, conv_b, wa_bd, b_a, wi_bd, b_i, lam)


def _att_kernel(q_ref, kw_ref, vw_ref, diag_ref, hm_ref, y_ref, bias_ref):
    lane_head0 = lax.broadcasted_iota(jnp.int32, (ATT_SUB, LANES), 1) < ATT_DH

    @pl.when((pl.program_id(1) == 0) & (pl.program_id(2) == 0))
    def _():
        r = lax.broadcasted_iota(jnp.int32, (ATT_SUB, ATT_WIN), 0)
        w = lax.broadcasted_iota(jnp.int32, (ATT_SUB, ATT_WIN), 1)
        kk = w - (r // CHUNK) * CHUNK
        in_band = (kk >= 0) & (kk < (ATT_LEFT_CHUNKS + 1) * CHUNK)
        for hh in range(ATT_PAIRS * HEADS_PER_PAIR):
            rows = jnp.broadcast_to(diag_ref[hh:hh + 1, :], (ATT_SUB, ATT_DIAG))
            skew = pltpu.roll(rows, ATT_DIAG - (ATT_SUB - 1), axis=1, stride=1, stride_axis=0)
            bias_ref[hh] = jnp.where(in_band, skew[:, 0:ATT_WIN], NEG_INF)

    def body(first):
        n_keys = ROW_TILE if first else 2 * ROW_TILE
        head0 = hm_ref[0:n_keys, :] > 0
        vh, qh = [], []
        for pair in range(ATT_PAIRS):
            lanes = slice(pair * LANES, (pair + 1) * LANES)
            v = vw_ref[0:n_keys, lanes]
            one16 = jnp.ones_like(v)
            vh.append((jnp.where(head0, v, one16), jnp.where(head0, one16, v)))
            q = q_ref[:, lanes]
            zero16 = jnp.zeros_like(q)
            qh.append((jnp.where(head0[0:ROW_TILE], q, zero16),
                       jnp.where(head0[0:ROW_TILE], zero16, q)))

        def window(j):
            if first:
                return slice(0, (j + 1) * ATT_SUB)
            return slice(j * ATT_SUB, j * ATT_SUB + ATT_WIN)

        def scores(pair, j, hh):
            win = window(j)
            n_win = win.stop - win.start
            rows = slice(j * ATT_SUB, (j + 1) * ATT_SUB)
            return (_dot_nt(qh[pair][hh][rows], kw_ref[win, pair * LANES:(pair + 1) * LANES])
                    + bias_ref[pair * HEADS_PER_PAIR + hh, :, ATT_WIN - n_win:ATT_WIN])

        units = [(pair, j, hh) for pair in range(ATT_PAIRS)
                 for j in range(ROW_TILE // ATT_SUB) for hh in range(HEADS_PER_PAIR)]
        n_units = len(units)
        s, p, outs = {}, {}, []
        for step in range(n_units + 2):
            if step < n_units:
                s[step] = scores(*units[step])
            if step >= 2:
                pair, j, hh = units[step - 2]
                of = _dot(p.pop(step - 2), vh[pair][hh][window(j)])
                outs.append(of / pltpu.roll(of, ATT_DH, axis=1))
                if hh == HEADS_PER_PAIR - 1:
                    rows = slice(j * ATT_SUB, (j + 1) * ATT_SUB)
                    y_ref[rows, pair * LANES:(pair + 1) * LANES] = jnp.where(
                        lane_head0, outs[0], outs[1]).astype(BF16)
                    outs = []
            if 1 <= step <= n_units:
                su = s.pop(step - 1)
                m = jnp.max(su, axis=-1, keepdims=True)
                p[step - 1] = jnp.exp2(su - m).astype(BF16)

    @pl.when(pl.program_id(2) == 0)
    def _():
        body(True)

    @pl.when(pl.program_id(2) > 0)
    def _():
        body(False)


def _attention(att_h, diag, batch):
    n = att_h.shape[0]
    seq = n // batch
    att3 = att_h.reshape(batch, seq, ATT_COLS)
    groups = ATT_WIDTH // (ATT_PAIRS * LANES)
    width = ATT_PAIRS * LANES
    head0 = jnp.asarray(
        np.broadcast_to(np.arange(LANES) < ATT_DH, (2 * ROW_TILE, LANES)).astype(np.float32), BF16)
    window = lambda off: pl.BlockSpec(
        (None, pl.Element(2 * ROW_TILE), pl.Element(width)),
        lambda g, b, i: (b, jnp.maximum(i - 1, 0) * ROW_TILE, (off + g * ATT_PAIRS) * LANES))
    y = pl.pallas_call(
        _att_kernel,
        grid=(groups, batch, seq // ROW_TILE),
        in_specs=[pl.BlockSpec((None, ROW_TILE, width), lambda g, b, i: (b, i, g)),
                  window(ATT_WIDTH // LANES), window(2 * ATT_WIDTH // LANES),
                  pl.BlockSpec((None, ATT_PAIRS * HEADS_PER_PAIR, ATT_DIAG),
                               lambda g, b, i: (g, 0, 0)),
                  pl.BlockSpec((2 * ROW_TILE, LANES), lambda g, b, i: (0, 0))],
        out_specs=pl.BlockSpec((None, ROW_TILE, width), lambda g, b, i: (b, i, g)),
        out_shape=jax.ShapeDtypeStruct((batch, seq, ATT_WIDTH), BF16),
        scratch_shapes=[pltpu.VMEM((ATT_PAIRS * HEADS_PER_PAIR, ATT_SUB, ATT_WIN), F32)],
        compiler_params=pltpu.CompilerParams(
            dimension_semantics=("arbitrary", "arbitrary", "arbitrary"),
            vmem_limit_bytes=VMEM_LIMIT_BYTES),
        name="band_attention",
    )(att3, att3, att3, diag, head0)
    return y.reshape(n, ATT_WIDTH)


def _attention_diag(rel_bias):
    n_diag = ATT_SUB + ATT_WIN - 1
    rel = ATT_LEFT_CHUNKS * CHUNK + (ATT_SUB - 1) - np.arange(n_diag)
    idx = np.clip(rel, -REL_CLIP, REL_CLIP) + REL_CLIP
    diag = jnp.pad(rel_bias.astype(F32)[:, idx] * LOG2_E, ((0, 0), (0, ATT_DIAG - n_diag)))
    return diag.reshape(-1, ATT_PAIRS * HEADS_PER_PAIR, ATT_DIAG)


def _post_kernel(yg_ref, ya_ref, yl_ref, x_ref, p_ref, wo_ref, g1_ref, b1_ref,
                 wg_ref, wu_ref, wd_ref, wpg_ref, wpp_ref, g2_ref, b2_ref, o_ref, act_ref):
    n_parts = POST_TILE // POST_ROWS
    rows = [slice(k * POST_ROWS, (k + 1) * POST_ROWS) for k in range(n_parts)]
    n_ff = D_FF // FF_TILE
    mix, x1, xb, ffn, ple = ({} for _ in range(5))

    def out_projection(k):
        mix[k] = (_dot(yg_ref[rows[k], :], wo_ref[0:GLA_WIDTH, :])
                  + _dot(ya_ref[rows[k], :], wo_ref[GLA_WIDTH:GLA_WIDTH + ATT_WIDTH, :])
                  + _dot(yl_ref[rows[k], :], wo_ref[GLA_WIDTH + ATT_WIDTH:D_MODEL, :]))

    def norm1(k):
        x1[k] = _layer_norm(DN_ALPHA * x_ref[rows[k], :] + mix.pop(k), g1_ref[...], b1_ref[...])
        xb[k] = x1[k].astype(BF16)

    def ffn_up(k, c):
        cols = slice(c * FF_TILE, (c + 1) * FF_TILE)
        gate = _dot(xb[k], wg_ref[:, cols])
        up = _dot(xb[k], wu_ref[:, cols])
        act_ref[k, :, cols] = (gate * jax.nn.sigmoid(gate) * up).astype(BF16)

    def ffn_down(k):
        ffn[k] = _dot(act_ref[k], wd_ref[...])
        ple[k] = (jax.nn.sigmoid(_dot(xb.pop(k), wpg_ref[...]))
                  * _dot(p_ref[rows[k], :].astype(BF16), wpp_ref[...]))

    def norm2(k):
        o_ref[rows[k], :] = _layer_norm(DN_ALPHA * x1.pop(k) + ffn.pop(k) + ple.pop(k),
                                        g2_ref[...], b2_ref[...])

    for k in range(n_parts):
        out_projection(k)
    norm1(0)
    for k in range(n_parts):
        for c in range(n_ff):
            ffn_up(k, c)
            if c == n_ff // 2 and k + 1 < n_parts:
                norm1(k + 1)
    for k in range(n_parts):
        ffn_down(k)
        if k > 0:
            norm2(k - 1)
    norm2(n_parts - 1)


def _post(layer, y_gla, y_att, y_lru, x2, p3, wo, g1, b1, wg, wu, wd, wpg, wpp, g2, b2):
    n = x2.shape[0]
    row = lambda c: pl.BlockSpec((POST_TILE, c), lambda i: (i, 0))
    full = lambda r, c: pl.BlockSpec((None, r, c), lambda i: (layer, 0, 0),
                                     pipeline_mode=pl.Buffered(1))
    return pl.pallas_call(
        _post_kernel,
        grid=(n // POST_TILE,),
        in_specs=[row(GLA_WIDTH), row(ATT_WIDTH), row(LRU_WIDTH), row(D_MODEL),
                  pl.BlockSpec((None, POST_TILE, PLE_DIM), lambda i: (layer, i, 0)),
                  full(D_MODEL, D_MODEL), full(1, D_MODEL), full(1, D_MODEL),
                  full(D_MODEL, D_FF), full(D_MODEL, D_FF), full(D_FF, D_MODEL),
                  full(D_MODEL, D_MODEL), full(PLE_DIM, D_MODEL),
                  full(1, D_MODEL), full(1, D_MODEL)],
        out_specs=row(D_MODEL),
        out_shape=jax.ShapeDtypeStruct((n, D_MODEL), F32),
        scratch_shapes=[pltpu.VMEM((POST_TILE // POST_ROWS, POST_ROWS, D_FF), BF16)],
        compiler_params=pltpu.CompilerParams(
            dimension_semantics=("arbitrary",), vmem_limit_bytes=VMEM_LIMIT_BYTES),
        name="out_proj_ffn",
    )(y_gla, y_att, y_lru, x2, p3, wo, g1, b1, wg, wu, wd, wpg, wpp, g2, b2)


def _block_diag(w):
    n_layers, g, c, d = w.shape
    eye = jnp.eye(g, dtype=w.dtype)
    return (eye[None, :, None, :, None] * w[:, :, :, None, :]).reshape(n_layers, g * c, g * d)


def _permute_w_in(w):
    wt = jnp.swapaxes(w, 1, 2)
    glr_lo = GLA_COLS
    glr_hi = GLA_COLS + GLA_GATE_RANK
    aq_hi = glr_hi + ATT_WIDTH
    pad = jnp.zeros((wt.shape[0], GLR_COLS - GLA_GATE_RANK, wt.shape[2]), w.dtype)
    return jnp.concatenate(
        [wt[:, :glr_lo], wt[:, glr_hi:aq_hi] * (ATT_DH ** -0.5 * LOG2_E), wt[:, aq_hi:],
         wt[:, glr_lo:glr_hi], pad], axis=1).astype(BF16)


def kernel(x, p, w_in, gla_w_lr_up, gla_b_lr_up, gla_norm_g, rel_bias, lru_conv_w, lru_conv_b,
           lru_w_a, lru_b_a, lru_w_i, lru_b_i, lru_lambda, w_out, ln1_g, ln1_b, w_ffn_gate,
           w_ffn_up, w_ffn_down, w_ple_gate, w_ple_proj, ln2_g, ln2_b):
    batch, seq, _ = x.shape
    n = batch * seq
    assert seq % ROW_TILE == 0
    x2 = x.reshape(n, D_MODEL)
    att_diag = _attention_diag(rel_bias)
    rows = lambda a: a.reshape(DEPTH, 1, -1).astype(F32)
    bf16 = lambda a: a.astype(BF16)

    mixer_params = (
        _permute_w_in(w_in),
        bf16(jnp.pad(gla_w_lr_up, ((0, 0), (0, GLR_COLS - GLA_GATE_RANK), (0, 0)))),
        rows(gla_b_lr_up), rows(jnp.tile(gla_norm_g, (1, GLA_HEADS))),
        lru_conv_w.astype(F32), rows(lru_conv_b),
        bf16(_block_diag(lru_w_a)), rows(lru_b_a),
        bf16(_block_diag(lru_w_i)), rows(lru_b_i), rows(lru_lambda))
    post_params = (
        bf16(w_out), rows(ln1_g), rows(ln1_b), bf16(w_ffn_gate), bf16(w_ffn_up), bf16(w_ffn_down),
        bf16(w_ple_gate), bf16(w_ple_proj), rows(ln2_g), rows(ln2_b))
    p3 = p.reshape(DEPTH, n, PLE_DIM)

    for layer in range(DEPTH):
        att_h, y_gla, y_lru = _mixer_in(layer, x2, *mixer_params, batch)
        y_att = _attention(att_h, att_diag, batch)
        x2 = _post(layer, y_gla, y_att, y_lru, x2, p3, *post_params)
    return x2.reshape(batch, seq, D_MODEL)
```

```python
import itertools
import math

import numpy as np
import jax
import jax.numpy as jnp
from jax import lax
from jax.experimental import pallas as pl
from jax.experimental.pallas import tpu as pltpu

F32 = jnp.float32
BF16 = jnp.bfloat16

D_MODEL = 1024
CHUNK = 64
GLA_HEADS = 4
GLA_DK = 64
GLA_WIDTH = 256
GLA_GATE_RANK = 16
GLA_GATE_TAU = 16.0
ATT_HEADS = 8
ATT_DH = 64
ATT_WIDTH = 512
ATT_LEFT_CHUNKS = 8
REL_CLIP = 256
LRU_WIDTH = 256
LRU_BLOCKS = 4
LRU_C = 8.0
CONV_WIDTH = 4
D_FF = 2816
PLE_DIM = 256
DEPTH = 2
DN_ALPHA = (2 * DEPTH) ** 0.25
LN_EPS = 1e-5
RMS_EPS = 1e-6
NEG_INF = -1e30
LOG2_E = math.log2(math.e)

LANES = 128
SUBLANES = 8
VMEM_LIMIT_BYTES = 56 * 1024 * 1024

GLA_COLS = 4 * GLA_WIDTH
ATT_COLS = 3 * ATT_WIDTH
LRU_COLS = 2 * LRU_WIDTH
GLR_COLS = LANES
ATT_OFF = GLA_COLS
LRU_OFF = ATT_OFF + ATT_COLS
GLR_OFF = LRU_OFF + LRU_COLS
D_IN_PAD = GLR_OFF + GLR_COLS

ROW_TILE = 512
MIX_TILE = 512
ATT_SUB = 256
ATT_WIN = ATT_SUB + ATT_LEFT_CHUNKS * CHUNK
ATT_DIAG = ATT_SUB + ATT_WIN
HEADS_PER_PAIR = LANES // ATT_DH
ATT_PAIRS = 4
FF_TILE = 256
PROJ_PIECE = 512
POST_TILE = 512
POST_ROWS = 256


def _dot(a, b):
    return jnp.dot(a, b, preferred_element_type=F32)


def _dot_nt(a, b):
    return lax.dot_general(a, b, (((1,), (1,)), ((), ())), preferred_element_type=F32)


def _dot_tn(a, b):
    return lax.dot_general(a, b, (((0,), (0,)), ((), ())), preferred_element_type=F32)


def _block_diag_mask(n, blk):
    r = lax.broadcasted_iota(jnp.int32, (n, n), 0) // blk
    c = lax.broadcasted_iota(jnp.int32, (n, n), 1) // blk
    return r == c


def _layer_norm(z, g, b):
    mu = jnp.mean(z, axis=-1, keepdims=True)
    zc = z - mu
    var = jnp.mean(zc * zc, axis=-1, keepdims=True)
    return zc * lax.rsqrt(var + LN_EPS) * g + b


def _softplus(x):
    return jnp.maximum(x, 0.0) + jnp.log1p(jnp.exp(-jnp.abs(x)))


def _shift_rows(x, d, fill, row_idx):
    return jnp.where(row_idx >= d, pltpu.roll(x, d, axis=0), fill)


def _gla_rows(get_h, wup_ref, bup_ref, ng_ref, bdm_ref, o_ref, state_ref, y_ref):
    h, lr = get_h()
    t = h.shape[0]
    z = _dot(lr.astype(BF16), wup_ref[...]) + bup_ref[...]
    la = -_softplus(-z) * (1.0 / GLA_GATE_TAU)
    la_hi = la.astype(BF16)
    la_lo = (la - la_hi.astype(F32)).astype(BF16)
    ri = lax.broadcasted_iota(jnp.int32, (CHUNK, CHUNK), 0)
    ci = lax.broadcasted_iota(jnp.int32, (CHUNK, CHUNK), 1)
    tri = jnp.where(ci <= ri, 1.0, 0.0).astype(BF16)
    bcum, refb, lastb = [], [], []
    for c in range(t // CHUNK):
        rows = slice(c * CHUNK, (c + 1) * CHUNK)
        bc = _dot(tri, la_hi[rows]) + _dot(tri, la_lo[rows])
        bcum.append(bc)
        refb.append(jnp.broadcast_to(bc[CHUNK // 2:CHUNK // 2 + 1], bc.shape))
        lastb.append(jnp.broadcast_to(bc[CHUNK - 1:CHUNK], bc.shape))
    bcum = jnp.concatenate(bcum, axis=0)
    refb = jnp.concatenate(refb, axis=0)
    lastb = jnp.concatenate(lastb, axis=0)
    yield

    q = h[:, 0:GLA_WIDTH] * (GLA_DK ** -0.5)
    k = h[:, GLA_WIDTH:2 * GLA_WIDTH]
    qe = (q * jnp.exp(bcum - refb)).astype(BF16)
    ke = (k * jnp.exp(refb - bcum)).astype(BF16)
    qd = (q * jnp.exp(bcum)).astype(BF16)
    kd = (k * jnp.exp(lastb - bcum)).astype(BF16)
    vb = h[:, 2 * GLA_WIDTH:3 * GLA_WIDTH].astype(BF16)
    yield

    bd = _block_diag_mask(GLA_WIDTH, GLA_DK)
    bd16 = bdm_ref[...] > 0
    zero16 = jnp.zeros((GLA_WIDTH, GLA_WIDTH), BF16)
    ri4 = lax.broadcasted_iota(jnp.int32, (CHUNK, GLA_WIDTH), 0)
    ci4 = lax.broadcasted_iota(jnp.int32, (CHUNK, GLA_WIDTH), 1) % CHUNK
    causal4 = ci4 <= ri4

    state = state_ref[...]
    for c in range(t // CHUNK):
        rows = slice(c * CHUNK, (c + 1) * CHUNK)
        kstack = jnp.where(bd16, jnp.concatenate([ke[rows]] * GLA_HEADS, axis=0), zero16)
        vstack = jnp.where(bd16, jnp.concatenate([vb[rows]] * GLA_HEADS, axis=0), zero16)
        attn = _dot_nt(qe[rows], kstack)
        attn = jnp.where(causal4, attn, 0.0).astype(BF16)
        o_ref[rows, :] = _dot(attn, vstack) + _dot_nt(qd[rows], state.astype(BF16))
        ut = _dot_tn(vb[rows], kd[rows])
        decay = jnp.exp(lastb[c * CHUNK:c * CHUNK + 1, :])
        state = state * decay + jnp.where(bd, ut, 0.0)
        if c % 2 == 1:
            yield
    state_ref[...] = state

    o = o_ref[...]
    group_mean = jnp.where(bd16, jnp.full_like(zero16, 1.0 / GLA_DK), zero16)
    ms = _dot((o * o).astype(BF16), group_mean)
    g = h[:, 3 * GLA_WIDTH:4 * GLA_WIDTH]
    y_ref[...] = (o * lax.rsqrt(ms + RMS_EPS) * ng_ref[...] * (g * jax.nn.sigmoid(g))).astype(BF16)


def _lru_rows(get_h, cw_ref, cb_ref, wa_ref, ba_ref, wi_ref, bi_ref, lam_ref,
              xtail, hcarry, xbuf, a_scr, u_scr, h_scr, hin_scr, y_ref):
    h = get_h()
    t = h.shape[0]
    n_groups = t // SUBLANES
    xbuf[0:SUBLANES, :] = xtail[...]
    xbuf[SUBLANES:SUBLANES + t, :] = h[:, 0:LRU_WIDTH]
    xc = cb_ref[...] + jnp.zeros((t, LRU_WIDTH), F32)
    for w in range(CONV_WIDTH):
        off = SUBLANES - (CONV_WIDTH - 1) + w
        xc = xc + cw_ref[w:w + 1, :] * xbuf[off:off + t, :]
    xtail[...] = xbuf[t:t + SUBLANES, :]
    yield

    xcb = xc.astype(BF16)
    r = jax.nn.sigmoid(_dot(xcb, wa_ref[...]) + ba_ref[...])
    ig = jax.nn.sigmoid(_dot(xcb, wi_ref[...]) + bi_ref[...])
    log_a = (-LRU_C) * r * _softplus(-lam_ref[...])
    a = jnp.exp(log_a)
    th = jnp.tanh(log_a)
    u = jnp.sqrt(-2.0 * th / (1.0 - th)) * (ig * xc)
    yield

    row_in_group = lax.broadcasted_iota(jnp.int32, (t, LRU_WIDTH), 0) % SUBLANES
    d = 1
    while d < SUBLANES:
        u = u + a * _shift_rows(u, d, 0.0, row_in_group)
        a = a * _shift_rows(a, d, 1.0, row_in_group)
        d *= 2
    lane_halves = [slice(v * LANES, (v + 1) * LANES) for v in range(LRU_WIDTH // LANES)]
    group_last = pl.ds(SUBLANES - 1, n_groups, stride=SUBLANES)
    for v, lanes in enumerate(lane_halves):
        a_scr[v] = a[:, lanes]
        u_scr[v] = u[:, lanes]
    yield
    a_g = jnp.concatenate([a_scr[v, group_last, :] for v in range(len(lane_halves))], axis=1)
    u_g = jnp.concatenate([u_scr[v, group_last, :] for v in range(len(lane_halves))], axis=1)
    group_idx = lax.broadcasted_iota(jnp.int32, (n_groups, LRU_WIDTH), 0)
    d = 1
    while d < n_groups:
        u_g = u_g + a_g * _shift_rows(u_g, d, 0.0, group_idx)
        a_g = a_g * _shift_rows(a_g, d, 1.0, group_idx)
        d *= 2
    h_prev = hcarry[0:1, :]
    h_end = u_g + a_g * h_prev
    hin_scr[...] = _shift_rows(h_end, 1, h_prev, group_idx)
    hcarry[...] = jnp.broadcast_to(h_end[n_groups - 1:n_groups, :], hcarry.shape)
    yield
    for g in range(n_groups):
        rows = slice(g * SUBLANES, (g + 1) * SUBLANES)
        for v, lanes in enumerate(lane_halves):
            h_scr[rows, lanes] = u_scr[v, rows, :] + a_scr[v, rows, :] * hin_scr[g:g + 1, lanes]
    y_ref[...] = (h_scr[...] * jax.nn.gelu(h[:, LRU_WIDTH:2 * LRU_WIDTH])).astype(BF16)


def _round_robin(*gens):
    pending = list(gens)
    while pending:
        alive = []
        for g in pending:
            if next(g, StopIteration) is not StopIteration:
                alive.append(g)
                yield
        pending = alive


def _mixer_in_kernel(x_ref, wt_ref, wup_ref, bup_ref, ng_ref, bdm_ref,
                     cw_ref, cb_ref, wa_ref, ba_ref, wi_ref, bi_ref, lam_ref,
                     att_ref, yg_ref, yl_ref,
                     w_ref, state_scr, xtail, hcarry, o_scr, xbuf, a_scr, u_scr, h_scr, hin_scr):
    @pl.when((pl.program_id(0) == 0) & (pl.program_id(1) == 0))
    def _():
        for c in range(0, D_IN_PAD, LANES):
            w_ref[:, c:c + LANES] = wt_ref[c:c + LANES, :].T

    @pl.when(pl.program_id(1) == 0)
    def _():
        state_scr[...] = jnp.zeros_like(state_scr)
        xtail[...] = jnp.zeros_like(xtail)
        hcarry[...] = jnp.zeros_like(hcarry)

    xb = x_ref[...].astype(BF16)
    out = {}

    def project(name, off, width):
        cols = []
        for c in range(0, width, PROJ_PIECE):
            n_cols = min(PROJ_PIECE, width - c)
            cols.append(_dot(xb, w_ref[:, off + c:off + c + n_cols]))
            if c + n_cols == width:
                out[name] = jnp.concatenate(cols, axis=1) if len(cols) > 1 else cols[0]
            yield

    def project_attention():
        for c in range(0, ATT_COLS, PROJ_PIECE):
            att_ref[:, c:c + PROJ_PIECE] = _dot(
                xb, w_ref[:, ATT_OFF + c:ATT_OFF + c + PROJ_PIECE]).astype(BF16)
            yield

    lru = _lru_rows(lambda: out["lru"], cw_ref, cb_ref, wa_ref, ba_ref, wi_ref, bi_ref, lam_ref,
                    xtail, hcarry, xbuf, a_scr, u_scr, h_scr, hin_scr, yl_ref)
    gla = _gla_rows(lambda: (out["gla"], out["glr"]), wup_ref, bup_ref, ng_ref, bdm_ref,
                    o_scr, state_scr, yg_ref)
    stages = (project("lru", LRU_OFF, LRU_COLS),
              _round_robin(itertools.chain(project("glr", GLR_OFF, GLR_COLS),
                                           project("gla", 0, GLA_COLS)), lru),
              _round_robin(project_attention(), gla))
    for _ in itertools.chain(*stages):
        pass


def _mixer_in(layer, x2, w_perm, wup_pad, bup, ng_row, conv_w, conv_b, wa_bd, b_a, wi_bd, b_i, lam,
              batch):
    n = x2.shape[0]
    nt = n // batch // MIX_TILE
    d = np.arange(GLA_WIDTH) // GLA_DK
    bdm = jnp.asarray((d[:, None] == d[None, :]).astype(np.float32), BF16)
    row = lambda c: pl.BlockSpec((MIX_TILE, c), lambda b, i: (b * nt + i, 0))
    full = lambda r, c: pl.BlockSpec((r, c), lambda b, i: (0, 0))
    of_layer = lambda r, c: pl.BlockSpec((None, r, c), lambda b, i: (layer, 0, 0))
    tile = lambda r, c: pltpu.VMEM((r, c), F32)
    lane_halves = pltpu.VMEM((LRU_WIDTH // LANES, MIX_TILE, LANES), F32)
    return pl.pallas_call(
        _mixer_in_kernel,
        grid=(batch, nt),
        in_specs=[row(D_MODEL),
                  pl.BlockSpec((None, D_IN_PAD, D_MODEL), lambda b, i: (layer, 0, 0),
                               pipeline_mode=pl.Buffered(1)),
                  of_layer(GLR_COLS, GLA_WIDTH), of_layer(1, GLA_WIDTH), of_layer(1, GLA_WIDTH),
                  full(GLA_WIDTH, GLA_WIDTH),
                  of_layer(CONV_WIDTH, LRU_WIDTH), of_layer(1, LRU_WIDTH),
                  of_layer(LRU_WIDTH, LRU_WIDTH), of_layer(1, LRU_WIDTH),
                  of_layer(LRU_WIDTH, LRU_WIDTH), of_layer(1, LRU_WIDTH), of_layer(1, LRU_WIDTH)],
        out_specs=[row(ATT_COLS), row(GLA_WIDTH), row(LRU_WIDTH)],
        out_shape=[jax.ShapeDtypeStruct((n, ATT_COLS), BF16),
                   jax.ShapeDtypeStruct((n, GLA_WIDTH), BF16),
                   jax.ShapeDtypeStruct((n, LRU_WIDTH), BF16)],
        scratch_shapes=[pltpu.VMEM((D_MODEL, D_IN_PAD), BF16),
                        pltpu.VMEM((GLA_WIDTH, GLA_WIDTH), F32),
                        pltpu.VMEM((SUBLANES, LRU_WIDTH), F32),
                        pltpu.VMEM((SUBLANES, LRU_WIDTH), F32),
                        tile(MIX_TILE, GLA_WIDTH),
                        tile(MIX_TILE + SUBLANES, LRU_WIDTH),
                        lane_halves, lane_halves,
                        tile(MIX_TILE, LRU_WIDTH),
                        tile(MIX_TILE // SUBLANES, LRU_WIDTH)],
        compiler_params=pltpu.CompilerParams(
            dimension_semantics=("arbitrary", "arbitrary"), vmem_limit_bytes=VMEM_LIMIT_BYTES),
        name="mixer_in",
    )(x2, w_perm, wup_pad, bup, ng_row, bdm, conv_w, conv_b, wa_bd, b_a, wi_bd, b_i, lam)


def _att_kernel(q_ref, kw_ref, vw_ref, diag_ref, hm_ref, y_ref, bias_ref):
    lane_head0 = lax.broadcasted_iota(jnp.int32, (ATT_SUB, LANES), 1) < ATT_DH

    @pl.when((pl.program_id(1) == 0) & (pl.program_id(2) == 0))
    def _():
        r = lax.broadcasted_iota(jnp.int32, (ATT_SUB, ATT_WIN), 0)
        w = lax.broadcasted_iota(jnp.int32, (ATT_SUB, ATT_WIN), 1)
        kk = w - (r // CHUNK) * CHUNK
        in_band = (kk >= 0) & (kk < (ATT_LEFT_CHUNKS + 1) * CHUNK)
        for hh in range(ATT_PAIRS * HEADS_PER_PAIR):
            rows = jnp.broadcast_to(diag_ref[hh:hh + 1, :], (ATT_SUB, ATT_DIAG))
            skew = pltpu.roll(rows, ATT_DIAG - (ATT_SUB - 1), axis=1, stride=1, stride_axis=0)
            bias_ref[hh] = jnp.where(in_band, skew[:, 0:ATT_WIN], NEG_INF)

    def body(first):
        n_keys = ROW_TILE if first else 2 * ROW_TILE
        head0 = hm_ref[0:n_keys, :] > 0
        vh, qh = [], []
        for pair in range(ATT_PAIRS):
            lanes = slice(pair * LANES, (pair + 1) * LANES)
            v = vw_ref[0:n_keys, lanes]
            one16 = jnp.ones_like(v)
            vh.append((jnp.where(head0, v, one16), jnp.where(head0, one16, v)))
            q = q_ref[:, lanes]
            zero16 = jnp.zeros_like(q)
            qh.append((jnp.where(head0[0:ROW_TILE], q, zero16),
                       jnp.where(head0[0:ROW_TILE], zero16, q)))

        def window(j):
            if first:
                return slice(0, (j + 1) * ATT_SUB)
            return slice(j * ATT_SUB, j * ATT_SUB + ATT_WIN)

        def scores(pair, j, hh):
            win = window(j)
            n_win = win.stop - win.start
            rows = slice(j * ATT_SUB, (j + 1) * ATT_SUB)
            return (_dot_nt(qh[pair][hh][rows], kw_ref[win, pair * LANES:(pair + 1) * LANES])
                    + bias_ref[pair * HEADS_PER_PAIR + hh, :, ATT_WIN - n_win:ATT_WIN])

        units = [(pair, j, hh) for pair in range(ATT_PAIRS)
                 for j in range(ROW_TILE // ATT_SUB) for hh in range(HEADS_PER_PAIR)]
        n_units = len(units)
        s, p, outs = {}, {}, []
        for step in range(n_units + 2):
            if step < n_units:
                s[step] = scores(*units[step])
            if step >= 2:
                pair, j, hh = units[step - 2]
                of = _dot(p.pop(step - 2), vh[pair][hh][window(j)])
                outs.append(of / pltpu.roll(of, ATT_DH, axis=1))
                if hh == HEADS_PER_PAIR - 1:
                    rows = slice(j * ATT_SUB, (j + 1) * ATT_SUB)
                    y_ref[rows, pair * LANES:(pair + 1) * LANES] = jnp.where(
                        lane_head0, outs[0], outs[1]).astype(BF16)
                    outs = []
            if 1 <= step <= n_units:
                su = s.pop(step - 1)
                m = jnp.max(su, axis=-1, keepdims=True)
                p[step - 1] = jnp.exp2(su - m).astype(BF16)

    @pl.when(pl.program_id(2) == 0)
    def _():
        body(True)

    @pl.when(pl.program_id(2) > 0)
    def _():
        body(False)


def _attention(att_h, diag, batch):
    n = att_h.shape[0]
    seq = n // batch
    att3 = att_h.reshape(batch, seq, ATT_COLS)
    groups = ATT_WIDTH // (ATT_PAIRS * LANES)
    width = ATT_PAIRS * LANES
    head0 = jnp.asarray(
        np.broadcast_to(np.arange(LANES) < ATT_DH, (2 * ROW_TILE, LANES)).astype(np.float32), BF16)
    window = lambda off: pl.BlockSpec(
        (None, pl.Element(2 * ROW_TILE), pl.Element(width)),
        lambda g, b, i: (b, jnp.maximum(i - 1, 0) * ROW_TILE, (off + g * ATT_PAIRS) * LANES))
    y = pl.pallas_call(
        _att_kernel,
        grid=(groups, batch, seq // ROW_TILE),
        in_specs=[pl.BlockSpec((None, ROW_TILE, width), lambda g, b, i: (b, i, g)),
                  window(ATT_WIDTH // LANES), window(2 * ATT_WIDTH // LANES),
                  pl.BlockSpec((None, ATT_PAIRS * HEADS_PER_PAIR, ATT_DIAG),
                               lambda g, b, i: (g, 0, 0)),
                  pl.BlockSpec((2 * ROW_TILE, LANES), lambda g, b, i: (0, 0))],
        out_specs=pl.BlockSpec((None, ROW_TILE, width), lambda g, b, i: (b, i, g)),
        out_shape=jax.ShapeDtypeStruct((batch, seq, ATT_WIDTH), BF16),
        scratch_shapes=[pltpu.VMEM((ATT_PAIRS * HEADS_PER_PAIR, ATT_SUB, ATT_WIN), F32)],
        compiler_params=pltpu.CompilerParams(
            dimension_semantics=("arbitrary", "arbitrary", "arbitrary"),
            vmem_limit_bytes=VMEM_LIMIT_BYTES),
        name="band_attention",
    )(att3, att3, att3, diag, head0)
    return y.reshape(n, ATT_WIDTH)


def _attention_diag(rel_bias):
    n_diag = ATT_SUB + ATT_WIN - 1
    rel = ATT_LEFT_CHUNKS * CHUNK + (ATT_SUB - 1) - np.arange(n_diag)
    idx = np.clip(rel, -REL_CLIP, REL_CLIP) + REL_CLIP
    diag = jnp.pad(rel_bias.astype(F32)[:, idx] * LOG2_E, ((0, 0), (0, ATT_DIAG - n_diag)))
    return diag.reshape(-1, ATT_PAIRS * HEADS_PER_PAIR, ATT_DIAG)


def _post_kernel(yg_ref, ya_ref, yl_ref, x_ref, p_ref, wo_ref, g1_ref, b1_ref,
                 wg_ref, wu_ref, wd_ref, wpg_ref, wpp_ref, g2_ref, b2_ref, o_ref, act_ref):
    n_parts = POST_TILE // POST_ROWS
    rows = [slice(k * POST_ROWS, (k + 1) * POST_ROWS) for k in range(n_parts)]
    n_ff = D_FF // FF_TILE
    mix, x1, xb, ffn, ple = ({} for _ in range(5))

    def out_projection(k):
        mix[k] = (_dot(yg_ref[rows[k], :], wo_ref[0:GLA_WIDTH, :])
                  + _dot(ya_ref[rows[k], :], wo_ref[GLA_WIDTH:GLA_WIDTH + ATT_WIDTH, :])
                  + _dot(yl_ref[rows[k], :], wo_ref[GLA_WIDTH + ATT_WIDTH:D_MODEL, :]))

    def norm1(k):
        x1[k] = _layer_norm(DN_ALPHA * x_ref[rows[k], :] + mix.pop(k), g1_ref[...], b1_ref[...])
        xb[k] = x1[k].astype(BF16)

    def ffn_up(k, c):
        cols = slice(c * FF_TILE, (c + 1) * FF_TILE)
        gate = _dot(xb[k], wg_ref[:, cols])
        up = _dot(xb[k], wu_ref[:, cols])
        act_ref[k, :, cols] = (gate * jax.nn.sigmoid(gate) * up).astype(BF16)

    def ffn_down(k):
        ffn[k] = _dot(act_ref[k], wd_ref[...])
        ple[k] = (jax.nn.sigmoid(_dot(xb.pop(k), wpg_ref[...]))
                  * _dot(p_ref[rows[k], :].astype(BF16), wpp_ref[...]))

    def norm2(k):
        o_ref[rows[k], :] = _layer_norm(DN_ALPHA * x1.pop(k) + ffn.pop(k) + ple.pop(k),
                                        g2_ref[...], b2_ref[...])

    for k in range(n_parts):
        out_projection(k)
    norm1(0)
    for k in range(n_parts):
        for c in range(n_ff):
            ffn_up(k, c)
            if c == n_ff // 2 and k + 1 < n_parts:
                norm1(k + 1)
    for k in range(n_parts):
        ffn_down(k)
        if k > 0:
            norm2(k - 1)
    norm2(n_parts - 1)


def _post(layer, y_gla, y_att, y_lru, x2, p3, wo, g1, b1, wg, wu, wd, wpg, wpp, g2, b2):
    n = x2.shape[0]
    row = lambda c: pl.BlockSpec((POST_TILE, c), lambda i: (i, 0))
    full = lambda r, c: pl.BlockSpec((None, r, c), lambda i: (layer, 0, 0),
                                     pipeline_mode=pl.Buffered(1))
    return pl.pallas_call(
        _post_kernel,
        grid=(n // POST_TILE,),
        in_specs=[row(GLA_WIDTH), row(ATT_WIDTH), row(LRU_WIDTH), row(D_MODEL),
                  pl.BlockSpec((None, POST_TILE, PLE_DIM), lambda i: (layer, i, 0)),
                  full(D_MODEL, D_MODEL), full(1, D_MODEL), full(1, D_MODEL),
                  full(D_MODEL, D_FF), full(D_MODEL, D_FF), full(D_FF, D_MODEL),
                  full(D_MODEL, D_MODEL), full(PLE_DIM, D_MODEL),
                  full(1, D_MODEL), full(1, D_MODEL)],
        out_specs=row(D_MODEL),
        out_shape=jax.ShapeDtypeStruct((n, D_MODEL), F32),
        scratch_shapes=[pltpu.VMEM((POST_TILE // POST_ROWS, POST_ROWS, D_FF), BF16)],
        compiler_params=pltpu.CompilerParams(
            dimension_semantics=("arbitrary",), vmem_limit_bytes=VMEM_LIMIT_BYTES),
        name="out_proj_ffn",
    )(y_gla, y_att, y_lru, x2, p3, wo, g1, b1, wg, wu, wd, wpg, wpp, g2, b2)


def _block_diag(w):
    n_layers, g, c, d = w.shape
    eye = jnp.eye(g, dtype=w.dtype)
    return (eye[None, :, None, :, None] * w[:, :, :, None, :]).reshape(n_layers, g * c, g * d)


def _permute_w_in(w):
    wt = jnp.swapaxes(w, 1, 2)
    glr_lo = GLA_COLS
    glr_hi = GLA_COLS + GLA_GATE_RANK
    aq_hi = glr_hi + ATT_WIDTH
    pad = jnp.zeros((wt.shape[0], GLR_COLS - GLA_GATE_RANK, wt.shape[2]), w.dtype)
    return jnp.concatenate(
        [wt[:, :glr_lo], wt[:, glr_hi:aq_hi] * (ATT_DH ** -0.5 * LOG2_E), wt[:, aq_hi:],
         wt[:, glr_lo:glr_hi], pad], axis=1).astype(BF16)


def kernel(x, p, w_in, gla_w_lr_up, gla_b_lr_up, gla_norm_g, rel_bias, lru_conv_w, lru_conv_b,
           lru_w_a, lru_b_a, lru_w_i, lru_b_i, lru_lambda, w_out, ln1_g, ln1_b, w_ffn_gate,
           w_ffn_up, w_ffn_down, w_ple_gate, w_ple_proj, ln2_g, ln2_b):
    batch, seq, _ = x.shape
    n = batch * seq
    assert seq % ROW_TILE == 0
    x2 = x.reshape(n, D_MODEL)
    att_diag = _attention_diag(rel_bias)
    rows = lambda a: a.reshape(DEPTH, 1, -1).astype(F32)
    bf16 = lambda a: a.astype(BF16)

    mixer_params = (
        _permute_w_in(w_in),
        bf16(jnp.pad(gla_w_lr_up, ((0, 0), (0, GLR_COLS - GLA_GATE_RANK), (0, 0)))),
        rows(gla_b_lr_up), rows(jnp.tile(gla_norm_g, (1, GLA_HEADS))),
        lru_conv_w.astype(F32), rows(lru_conv_b),
        bf16(_block_diag(lru_w_a)), rows(lru_b_a),
        bf16(_block_diag(lru_w_i)), rows(lru_b_i), rows(lru_lambda))
    post_params = (
        bf16(w_out), rows(ln1_g), rows(ln1_b), bf16(w_ffn_gate), bf16(w_ffn_up), bf16(w_ffn_down),
        bf16(w_ple_gate), bf16(w_ple_proj), rows(ln2_g), rows(ln2_b))
    p3 = p.reshape(DEPTH, n, PLE_DIM)

    for layer in range(DEPTH):
        att_h, y_gla, y_lru = _mixer_in(layer, x2, *mixer_params, batch)
        y_att = _attention(att_h, att_diag, batch)
        x2 = _post(layer, y_gla, y_att, y_lru, x2, p3, *post_params)
    return x2.reshape(batch, seq, D_MODEL)
```

```python
import itertools
import math

import numpy as np
import jax
import jax.numpy as jnp
from jax import lax
from jax.experimental import pallas as pl
from jax.experimental.pallas import tpu as pltpu

F32 = jnp.float32
BF16 = jnp.bfloat16

D_MODEL = 1024
CHUNK = 64
GLA_HEADS = 4
GLA_DK = 64
GLA_WIDTH = 256
GLA_GATE_RANK = 16
GLA_GATE_TAU = 16.0
ATT_HEADS = 8
ATT_DH = 64
ATT_WIDTH = 512
ATT_LEFT_CHUNKS = 8
REL_CLIP = 256
LRU_WIDTH = 256
LRU_BLOCKS = 4
LRU_C = 8.0
CONV_WIDTH = 4
D_FF = 2816
PLE_DIM = 256
DEPTH = 2
DN_ALPHA = (2 * DEPTH) ** 0.25
LN_EPS = 1e-5
RMS_EPS = 1e-6
NEG_INF = -1e30
LOG2_E = math.log2(math.e)

LANES = 128
SUBLANES = 8
VMEM_LIMIT_BYTES = 56 * 1024 * 1024

GLA_COLS = 4 * GLA_WIDTH
ATT_COLS = 3 * ATT_WIDTH
LRU_COLS = 2 * LRU_WIDTH
GLR_COLS = LANES
ATT_OFF = GLA_COLS
LRU_OFF = ATT_OFF + ATT_COLS
GLR_OFF = LRU_OFF + LRU_COLS
D_IN_PAD = GLR_OFF + GLR_COLS

ROW_TILE = 512
MIX_TILE = 512
ATT_SUB = 256
ATT_WIN = ATT_SUB + ATT_LEFT_CHUNKS * CHUNK
ATT_DIAG = ATT_SUB + ATT_WIN
HEADS_PER_PAIR = LANES // ATT_DH
ATT_PAIRS = 4
FF_TILE = 256
PROJ_PIECE = 512
POST_TILE = 1024
POST_ROWS = 256


def _dot(a, b):
    return jnp.dot(a, b, preferred_element_type=F32)


def _dot_nt(a, b):
    return lax.dot_general(a, b, (((1,), (1,)), ((), ())), preferred_element_type=F32)


def _dot_tn(a, b):
    return lax.dot_general(a, b, (((0,), (0,)), ((), ())), preferred_element_type=F32)


def _block_diag_mask(n, blk):
    r = lax.broadcasted_iota(jnp.int32, (n, n), 0) // blk
    c = lax.broadcasted_iota(jnp.int32, (n, n), 1) // blk
    return r == c


def _layer_norm(z, g, b):
    mu = jnp.mean(z, axis=-1, keepdims=True)
    zc = z - mu
    var = jnp.mean(zc * zc, axis=-1, keepdims=True)
    return zc * lax.rsqrt(var + LN_EPS) * g + b


def _softplus(x):
    return jnp.maximum(x, 0.0) + jnp.log1p(jnp.exp(-jnp.abs(x)))


def _shift_rows(x, d, fill, row_idx):
    return jnp.where(row_idx >= d, pltpu.roll(x, d, axis=0), fill)


def _gla_rows(get_h, wup_ref, bup_ref, ng_ref, bdm_ref, o_ref, state_ref, y_ref):
    h, lr = get_h()
    t = h.shape[0]
    z = _dot(lr.astype(BF16), wup_ref[...]) + bup_ref[...]
    la = -_softplus(-z) * (1.0 / GLA_GATE_TAU)
    la_hi = la.astype(BF16)
    la_lo = (la - la_hi.astype(F32)).astype(BF16)
    ri = lax.broadcasted_iota(jnp.int32, (CHUNK, CHUNK), 0)
    ci = lax.broadcasted_iota(jnp.int32, (CHUNK, CHUNK), 1)
    tri = jnp.where(ci <= ri, 1.0, 0.0).astype(BF16)
    bcum, refb, lastb = [], [], []
    for c in range(t // CHUNK):
        rows = slice(c * CHUNK, (c + 1) * CHUNK)
        bc = _dot(tri, la_hi[rows]) + _dot(tri, la_lo[rows])
        bcum.append(bc)
        refb.append(jnp.broadcast_to(bc[CHUNK // 2:CHUNK // 2 + 1], bc.shape))
        lastb.append(jnp.broadcast_to(bc[CHUNK - 1:CHUNK], bc.shape))
    bcum = jnp.concatenate(bcum, axis=0)
    refb = jnp.concatenate(refb, axis=0)
    lastb = jnp.concatenate(lastb, axis=0)
    yield

    q = h[:, 0:GLA_WIDTH] * (GLA_DK ** -0.5)
    k = h[:, GLA_WIDTH:2 * GLA_WIDTH]
    qe = (q * jnp.exp(bcum - refb)).astype(BF16)
    ke = (k * jnp.exp(refb - bcum)).astype(BF16)
    qd = (q * jnp.exp(bcum)).astype(BF16)
    kd = (k * jnp.exp(lastb - bcum)).astype(BF16)
    vb = h[:, 2 * GLA_WIDTH:3 * GLA_WIDTH].astype(BF16)
    yield

    bd = _block_diag_mask(GLA_WIDTH, GLA_DK)
    bd16 = bdm_ref[...] > 0
    zero16 = jnp.zeros((GLA_WIDTH, GLA_WIDTH), BF16)
    ri4 = lax.broadcasted_iota(jnp.int32, (CHUNK, GLA_WIDTH), 0)
    ci4 = lax.broadcasted_iota(jnp.int32, (CHUNK, GLA_WIDTH), 1) % CHUNK
    causal4 = ci4 <= ri4

    state = state_ref[...]
    for c in range(t // CHUNK):
        rows = slice(c * CHUNK, (c + 1) * CHUNK)
        kstack = jnp.where(bd16, jnp.concatenate([ke[rows]] * GLA_HEADS, axis=0), zero16)
        vstack = jnp.where(bd16, jnp.concatenate([vb[rows]] * GLA_HEADS, axis=0), zero16)
        attn = _dot_nt(qe[rows], kstack)
        attn = jnp.where(causal4, attn, 0.0).astype(BF16)
        o_ref[rows, :] = _dot(attn, vstack) + _dot_nt(qd[rows], state.astype(BF16))
        ut = _dot_tn(vb[rows], kd[rows])
        decay = jnp.exp(lastb[c * CHUNK:c * CHUNK + 1, :])
        state = state * decay + jnp.where(bd, ut, 0.0)
        if c % 2 == 1:
            yield
    state_ref[...] = state

    o = o_ref[...]
    group_mean = jnp.where(bd16, jnp.full_like(zero16, 1.0 / GLA_DK), zero16)
    ms = _dot((o * o).astype(BF16), group_mean)
    g = h[:, 3 * GLA_WIDTH:4 * GLA_WIDTH]
    y_ref[...] = (o * lax.rsqrt(ms + RMS_EPS) * ng_ref[...] * (g * jax.nn.sigmoid(g))).astype(BF16)


def _lru_rows(get_h, cw_ref, cb_ref, wa_ref, ba_ref, wi_ref, bi_ref, lam_ref,
              xtail, hcarry, xbuf, a_scr, u_scr, h_scr, hin_scr, y_ref):
    h = get_h()
    t = h.shape[0]
    n_groups = t // SUBLANES
    xbuf[0:SUBLANES, :] = xtail[...]
    xbuf[SUBLANES:SUBLANES + t, :] = h[:, 0:LRU_WIDTH]
    xc = cb_ref[...] + jnp.zeros((t, LRU_WIDTH), F32)
    for w in range(CONV_WIDTH):
        off = SUBLANES - (CONV_WIDTH - 1) + w
        xc = xc + cw_ref[w:w + 1, :] * xbuf[off:off + t, :]
    xtail[...] = xbuf[t:t + SUBLANES, :]
    yield

    xcb = xc.astype(BF16)
    r = jax.nn.sigmoid(_dot(xcb, wa_ref[...]) + ba_ref[...])
    ig = jax.nn.sigmoid(_dot(xcb, wi_ref[...]) + bi_ref[...])
    log_a = (-LRU_C) * r * _softplus(-lam_ref[...])
    a = jnp.exp(log_a)
    th = jnp.tanh(log_a)
    u = jnp.sqrt(-2.0 * th / (1.0 - th)) * (ig * xc)
    yield

    row_in_group = lax.broadcasted_iota(jnp.int32, (t, LRU_WIDTH), 0) % SUBLANES
    d = 1
    while d < SUBLANES:
        u = u + a * _shift_rows(u, d, 0.0, row_in_group)
        a = a * _shift_rows(a, d, 1.0, row_in_group)
        d *= 2
    lane_halves = [slice(v * LANES, (v + 1) * LANES) for v in range(LRU_WIDTH // LANES)]
    group_last = pl.ds(SUBLANES - 1, n_groups, stride=SUBLANES)
    for v, lanes in enumerate(lane_halves):
        a_scr[v] = a[:, lanes]
        u_scr[v] = u[:, lanes]
    yield
    a_g = jnp.concatenate([a_scr[v, group_last, :] for v in range(len(lane_halves))], axis=1)
    u_g = jnp.concatenate([u_scr[v, group_last, :] for v in range(len(lane_halves))], axis=1)
    group_idx = lax.broadcasted_iota(jnp.int32, (n_groups, LRU_WIDTH), 0)
    d = 1
    while d < n_groups:
        u_g = u_g + a_g * _shift_rows(u_g, d, 0.0, group_idx)
        a_g = a_g * _shift_rows(a_g, d, 1.0, group_idx)
        d *= 2
    h_prev = hcarry[0:1, :]
    h_end = u_g + a_g * h_prev
    hin_scr[...] = _shift_rows(h_end, 1, h_prev, group_idx)
    hcarry[...] = jnp.broadcast_to(h_end[n_groups - 1:n_groups, :], hcarry.shape)
    yield
    for g in range(n_groups):
        rows = slice(g * SUBLANES, (g + 1) * SUBLANES)
        for v, lanes in enumerate(lane_halves):
            h_scr[rows, lanes] = u_scr[v, rows, :] + a_scr[v, rows, :] * hin_scr[g:g + 1, lanes]
    y_ref[...] = (h_scr[...] * jax.nn.gelu(h[:, LRU_WIDTH:2 * LRU_WIDTH])).astype(BF16)


def _round_robin(*gens):
    pending = list(gens)
    while pending:
        alive = []
        for g in pending:
            if next(g, StopIteration) is not StopIteration:
                alive.append(g)
                yield
        pending = alive


def _mixer_in_kernel(x_ref, wt_ref, wup_ref, bup_ref, ng_ref, bdm_ref,
                     cw_ref, cb_ref, wa_ref, ba_ref, wi_ref, bi_ref, lam_ref,
                     att_ref, yg_ref, yl_ref,
                     w_ref, state_scr, xtail, hcarry, o_scr, xbuf, a_scr, u_scr, h_scr, hin_scr):
    @pl.when((pl.program_id(0) == 0) & (pl.program_id(1) == 0))
    def _():
        for c in range(0, D_IN_PAD, LANES):
            w_ref[:, c:c + LANES] = wt_ref[c:c + LANES, :].T

    @pl.when(pl.program_id(1) == 0)
    def _():
        state_scr[...] = jnp.zeros_like(state_scr)
        xtail[...] = jnp.zeros_like(xtail)
        hcarry[...] = jnp.zeros_like(hcarry)

    xb = x_ref[...].astype(BF16)
    out = {}

    def project(name, off, width):
        cols = []
        for c in range(0, width, PROJ_PIECE):
            n_cols = min(PROJ_PIECE, width - c)
            cols.append(_dot(xb, w_ref[:, off + c:off + c + n_cols]))
            if c + n_cols == width:
                out[name] = jnp.concatenate(cols, axis=1) if len(cols) > 1 else cols[0]
            yield

    def project_attention():
        for c in range(0, ATT_COLS, PROJ_PIECE):
            att_ref[:, c:c + PROJ_PIECE] = _dot(
                xb, w_ref[:, ATT_OFF + c:ATT_OFF + c + PROJ_PIECE]).astype(BF16)
            yield

    lru = _lru_rows(lambda: out["lru"], cw_ref, cb_ref, wa_ref, ba_ref, wi_ref, bi_ref, lam_ref,
                    xtail, hcarry, xbuf, a_scr, u_scr, h_scr, hin_scr, yl_ref)
    gla = _gla_rows(lambda: (out["gla"], out["glr"]), wup_ref, bup_ref, ng_ref, bdm_ref,
                    o_scr, state_scr, yg_ref)
    stages = (project("lru", LRU_OFF, LRU_COLS),
              _round_robin(itertools.chain(project("glr", GLR_OFF, GLR_COLS),
                                           project("gla", 0, GLA_COLS)), lru),
              _round_robin(project_attention(), gla))
    for _ in itertools.chain(*stages):
        pass


def _mixer_in(layer, x2, w_perm, wup_pad, bup, ng_row, conv_w, conv_b, wa_bd, b_a, wi_bd, b_i, lam,
              batch):
    n = x2.shape[0]
    nt = n // batch // MIX_TILE
    d = np.arange(GLA_WIDTH) // GLA_DK
    bdm = jnp.asarray((d[:, None] == d[None, :]).astype(np.float32), BF16)
    row = lambda c: pl.BlockSpec((MIX_TILE, c), lambda b, i: (b * nt + i, 0))
    full = lambda r, c: pl.BlockSpec((r, c), lambda b, i: (0, 0))
    of_layer = lambda r, c: pl.BlockSpec((None, r, c), lambda b, i: (layer, 0, 0))
    tile = lambda r, c: pltpu.VMEM((r, c), F32)
    lane_halves = pltpu.VMEM((LRU_WIDTH // LANES, MIX_TILE, LANES), F32)
    return pl.pallas_call(
        _mixer_in_kernel,
        grid=(batch, nt),
        in_specs=[row(D_MODEL),
                  pl.BlockSpec((None, D_IN_PAD, D_MODEL), lambda b, i: (layer, 0, 0),
                               pipeline_mode=pl.Buffered(1)),
                  of_layer(GLR_COLS, GLA_WIDTH), of_layer(1, GLA_WIDTH), of_layer(1, GLA_WIDTH),
                  full(GLA_WIDTH, GLA_WIDTH),
                  of_layer(CONV_WIDTH, LRU_WIDTH), of_layer(1, LRU_WIDTH),
                  of_layer(LRU_WIDTH, LRU_WIDTH), of_layer(1, LRU_WIDTH),
                  of_layer(LRU_WIDTH, LRU_WIDTH), of_layer(1, LRU_WIDTH), of_layer(1, LRU_WIDTH)],
        out_specs=[row(ATT_COLS), row(GLA_WIDTH), row(LRU_WIDTH)],
        out_shape=[jax.ShapeDtypeStruct((n, ATT_COLS), BF16),
                   jax.ShapeDtypeStruct((n, GLA_WIDTH), BF16),
                   jax.ShapeDtypeStruct((n, LRU_WIDTH), BF16)],
        scratch_shapes=[pltpu.VMEM((D_MODEL, D_IN_PAD), BF16),
                        pltpu.VMEM((GLA_WIDTH, GLA_WIDTH), F32),
                        pltpu.VMEM((SUBLANES, LRU_WIDTH), F32),
                        pltpu.VMEM((SUBLANES, LRU_WIDTH), F32),
                        tile(MIX_TILE, GLA_WIDTH),
                        tile(MIX_TILE + SUBLANES, LRU_WIDTH),
                        lane_halves, lane_halves,
                        tile(MIX_TILE, LRU_WIDTH),
                        tile(MIX_TILE // SUBLANES, LRU_WIDTH)],
        compiler_params=pltpu.CompilerParams(
            dimension_semantics=("arbitrary", "arbitrary"), vmem_limit_bytes=VMEM_LIMIT_BYTES),
        name="mixer_in",
    )(x2, w_perm, wup_pad, bup, ng_row, bdm, conv_w, conv_b, wa_bd, b_a, wi_bd, b_i, lam)


def _att_kernel(q_ref, kw_ref, vw_ref, diag_ref, hm_ref, y_ref, bias_ref):
    lane_head0 = lax.broadcasted_iota(jnp.int32, (ATT_SUB, LANES), 1) < ATT_DH

    @pl.when((pl.program_id(1) == 0) & (pl.program_id(2) == 0))
    def _():
        r = lax.broadcasted_iota(jnp.int32, (ATT_SUB, ATT_WIN), 0)
        w = lax.broadcasted_iota(jnp.int32, (ATT_SUB, ATT_WIN), 1)
        kk = w - (r // CHUNK) * CHUNK
        in_band = (kk >= 0) & (kk < (ATT_LEFT_CHUNKS + 1) * CHUNK)
        for hh in range(ATT_PAIRS * HEADS_PER_PAIR):
            rows = jnp.broadcast_to(diag_ref[hh:hh + 1, :], (ATT_SUB, ATT_DIAG))
            skew = pltpu.roll(rows, ATT_DIAG - (ATT_SUB - 1), axis=1, stride=1, stride_axis=0)
            bias_ref[hh] = jnp.where(in_band, skew[:, 0:ATT_WIN], NEG_INF)

    def body(first):
        n_keys = ROW_TILE if first else 2 * ROW_TILE
        head0 = hm_ref[0:n_keys, :] > 0
        vh, qh = [], []
        for pair in range(ATT_PAIRS):
            lanes = slice(pair * LANES, (pair + 1) * LANES)
            v = vw_ref[0:n_keys, lanes]
            one16 = jnp.ones_like(v)
            vh.append((jnp.where(head0, v, one16), jnp.where(head0, one16, v)))
            q = q_ref[:, lanes]
            zero16 = jnp.zeros_like(q)
            qh.append((jnp.where(head0[0:ROW_TILE], q, zero16),
                       jnp.where(head0[0:ROW_TILE], zero16, q)))

        def window(j):
            if first:
                return slice(0, (j + 1) * ATT_SUB)
            return slice(j * ATT_SUB, j * ATT_SUB + ATT_WIN)

        def scores(pair, j, hh):
            win = window(j)
            n_win = win.stop - win.start
            rows = slice(j * ATT_SUB, (j + 1) * ATT_SUB)
            return (_dot_nt(qh[pair][hh][rows], kw_ref[win, pair * LANES:(pair + 1) * LANES])
                    + bias_ref[pair * HEADS_PER_PAIR + hh, :, ATT_WIN - n_win:ATT_WIN])

        units = [(pair, j, hh) for pair in range(ATT_PAIRS)
                 for j in range(ROW_TILE // ATT_SUB) for hh in range(HEADS_PER_PAIR)]
        n_units = len(units)
        s, p, outs = {}, {}, []
        for step in range(n_units + 2):
            if step < n_units:
                s[step] = scores(*units[step])
            if step >= 2:
                pair, j, hh = units[step - 2]
                of = _dot(p.pop(step - 2), vh[pair][hh][window(j)])
                outs.append(of / pltpu.roll(of, ATT_DH, axis=1))
                if hh == HEADS_PER_PAIR - 1:
                    rows = slice(j * ATT_SUB, (j + 1) * ATT_SUB)
                    y_ref[rows, pair * LANES:(pair + 1) * LANES] = jnp.where(
                        lane_head0, outs[0], outs[1]).astype(BF16)
                    outs = []
            if 1 <= step <= n_units:
                su = s.pop(step - 1)
                m = jnp.max(su, axis=-1, keepdims=True)
                p[step - 1] = jnp.exp2(su - m).astype(BF16)

    @pl.when(pl.program_id(2) == 0)
    def _():
        body(True)

    @pl.when(pl.program_id(2) > 0)
    def _():
        body(False)


def _attention(att_h, diag, batch):
    n = att_h.shape[0]
    seq = n // batch
    att3 = att_h.reshape(batch, seq, ATT_COLS)
    groups = ATT_WIDTH // (ATT_PAIRS * LANES)
    width = ATT_PAIRS * LANES
    head0 = jnp.asarray(
        np.broadcast_to(np.arange(LANES) < ATT_DH, (2 * ROW_TILE, LANES)).astype(np.float32), BF16)
    window = lambda off: pl.BlockSpec(
        (None, pl.Element(2 * ROW_TILE), pl.Element(width)),
        lambda g, b, i: (b, jnp.maximum(i - 1, 0) * ROW_TILE, (off + g * ATT_PAIRS) * LANES))
    y = pl.pallas_call(
        _att_kernel,
        grid=(groups, batch, seq // ROW_TILE),
        in_specs=[pl.BlockSpec((None, ROW_TILE, width), lambda g, b, i: (b, i, g)),
                  window(ATT_WIDTH // LANES), window(2 * ATT_WIDTH // LANES),
                  pl.BlockSpec((None, ATT_PAIRS * HEADS_PER_PAIR, ATT_DIAG),
                               lambda g, b, i: (g, 0, 0)),
                  pl.BlockSpec((2 * ROW_TILE, LANES), lambda g, b, i: (0, 0))],
        out_specs=pl.BlockSpec((None, ROW_TILE, width), lambda g, b, i: (b, i, g)),
        out_shape=jax.ShapeDtypeStruct((batch, seq, ATT_WIDTH), BF16),
        scratch_shapes=[pltpu.VMEM((ATT_PAIRS * HEADS_PER_PAIR, ATT_SUB, ATT_WIN), F32)],
        compiler_params=pltpu.CompilerParams(
            dimension_semantics=("arbitrary", "arbitrary", "arbitrary"),
            vmem_limit_bytes=VMEM_LIMIT_BYTES),
        name="band_attention",
    )(att3, att3, att3, diag, head0)
    return y.reshape(n, ATT_WIDTH)


def _attention_diag(rel_bias):
    n_diag = ATT_SUB + ATT_WIN - 1
    rel = ATT_LEFT_CHUNKS * CHUNK + (ATT_SUB - 1) - np.arange(n_diag)
    idx = np.clip(rel, -REL_CLIP, REL_CLIP) + REL_CLIP
    diag = jnp.pad(rel_bias.astype(F32)[:, idx] * LOG2_E, ((0, 0), (0, ATT_DIAG - n_diag)))
    return diag.reshape(-1, ATT_PAIRS * HEADS_PER_PAIR, ATT_DIAG)


def _post_kernel(yg_ref, ya_ref, yl_ref, x_ref, p_ref, wo_ref, g1_ref, b1_ref,
                 wg_ref, wu_ref, wd_ref, wpg_ref, wpp_ref, g2_ref, b2_ref, o_ref, act_ref):
    n_parts = POST_TILE // POST_ROWS
    rows = [slice(k * POST_ROWS, (k + 1) * POST_ROWS) for k in range(n_parts)]
    n_ff = D_FF // FF_TILE
    mix, x1, xb, ffn, ple = ({} for _ in range(5))

    def out_projection(k):
        mix[k] = (_dot(yg_ref[rows[k], :], wo_ref[0:GLA_WIDTH, :])
                  + _dot(ya_ref[rows[k], :], wo_ref[GLA_WIDTH:GLA_WIDTH + ATT_WIDTH, :])
                  + _dot(yl_ref[rows[k], :], wo_ref[GLA_WIDTH + ATT_WIDTH:D_MODEL, :]))

    def norm1(k):
        x1[k] = _layer_norm(DN_ALPHA * x_ref[rows[k], :] + mix.pop(k), g1_ref[...], b1_ref[...])
        xb[k] = x1[k].astype(BF16)

    def ffn_up(k, c):
        cols = slice(c * FF_TILE, (c + 1) * FF_TILE)
        gate = _dot(xb[k], wg_ref[:, cols])
        up = _dot(xb[k], wu_ref[:, cols])
        act_ref[k % 2, :, cols] = (gate * jax.nn.sigmoid(gate) * up).astype(BF16)

    def ffn_down(k):
        ffn[k] = _dot(act_ref[k % 2], wd_ref[...])
        ple[k] = (jax.nn.sigmoid(_dot(xb.pop(k), wpg_ref[...]))
                  * _dot(p_ref[rows[k], :].astype(BF16), wpp_ref[...]))

    def norm2(k):
        o_ref[rows[k], :] = _layer_norm(DN_ALPHA * x1.pop(k) + ffn.pop(k) + ple.pop(k),
                                        g2_ref[...], b2_ref[...])

    out_projection(0)
    norm1(0)
    for k in range(n_parts + 2):
        if k < n_parts:
            if k + 1 < n_parts:
                out_projection(k + 1)
            for c in range(n_ff):
                ffn_up(k, c)
                if c == n_ff // 2 and k + 1 < n_parts:
                    norm1(k + 1)
        if 1 <= k <= n_parts:
            ffn_down(k - 1)
        if k >= 2:
            norm2(k - 2)


def _post(layer, y_gla, y_att, y_lru, x2, p3, wo, g1, b1, wg, wu, wd, wpg, wpp, g2, b2):
    n = x2.shape[0]
    row = lambda c: pl.BlockSpec((POST_TILE, c), lambda i: (i, 0))
    full = lambda r, c: pl.BlockSpec((None, r, c), lambda i: (layer, 0, 0),
                                     pipeline_mode=pl.Buffered(1))
    return pl.pallas_call(
        _post_kernel,
        grid=(n // POST_TILE,),
        in_specs=[row(GLA_WIDTH), row(ATT_WIDTH), row(LRU_WIDTH), row(D_MODEL),
                  pl.BlockSpec((None, POST_TILE, PLE_DIM), lambda i: (layer, i, 0)),
                  full(D_MODEL, D_MODEL), full(1, D_MODEL), full(1, D_MODEL),
                  full(D_MODEL, D_FF), full(D_MODEL, D_FF), full(D_FF, D_MODEL),
                  full(D_MODEL, D_MODEL), full(PLE_DIM, D_MODEL),
                  full(1, D_MODEL), full(1, D_MODEL)],
        out_specs=row(D_MODEL),
        out_shape=jax.ShapeDtypeStruct((n, D_MODEL), F32),
        scratch_shapes=[pltpu.VMEM((2, POST_ROWS, D_FF), BF16)],
        compiler_params=pltpu.CompilerParams(
            dimension_semantics=("arbitrary",), vmem_limit_bytes=VMEM_LIMIT_BYTES),
        name="out_proj_ffn",
    )(y_gla, y_att, y_lru, x2, p3, wo, g1, b1, wg, wu, wd, wpg, wpp, g2, b2)


def _block_diag(w):
    n_layers, g, c, d = w.shape
    eye = jnp.eye(g, dtype=w.dtype)
    return (eye[None, :, None, :, None] * w[:, :, :, None, :]).reshape(n_layers, g * c, g * d)


def _permute_w_in(w):
    wt = jnp.swapaxes(w, 1, 2)
    glr_lo = GLA_COLS
    glr_hi = GLA_COLS + GLA_GATE_RANK
    aq_hi = glr_hi + ATT_WIDTH
    pad = jnp.zeros((wt.shape[0], GLR_COLS - GLA_GATE_RANK, wt.shape[2]), w.dtype)
    return jnp.concatenate(
        [wt[:, :glr_lo], wt[:, glr_hi:aq_hi] * (ATT_DH ** -0.5 * LOG2_E), wt[:, aq_hi:],
         wt[:, glr_lo:glr_hi], pad], axis=1).astype(BF16)


def kernel(x, p, w_in, gla_w_lr_up, gla_b_lr_up, gla_norm_g, rel_bias, lru_conv_w, lru_conv_b,
           lru_w_a, lru_b_a, lru_w_i, lru_b_i, lru_lambda, w_out, ln1_g, ln1_b, w_ffn_gate,
           w_ffn_up, w_ffn_down, w_ple_gate, w_ple_proj, ln2_g, ln2_b):
    batch, seq, _ = x.shape
    n = batch * seq
    assert seq % ROW_TILE == 0
    x2 = x.reshape(n, D_MODEL)
    att_diag = _attention_diag(rel_bias)
    rows = lambda a: a.reshape(DEPTH, 1, -1).astype(F32)
    bf16 = lambda a: a.astype(BF16)

    mixer_params = (
        _permute_w_in(w_in),
        bf16(jnp.pad(gla_w_lr_up, ((0, 0), (0, GLR_COLS - GLA_GATE_RANK), (0, 0)))),
        rows(gla_b_lr_up), rows(jnp.tile(gla_norm_g, (1, GLA_HEADS))),
        lru_conv_w.astype(F32), rows(lru_conv_b),
        bf16(_block_diag(lru_w_a)), rows(lru_b_a),
        bf16(_block_diag(lru_w_i)), rows(lru_b_i), rows(lru_lambda))
    post_params = (
        bf16(w_out), rows(ln1_g), rows(ln1_b), bf16(w_ffn_gate), bf16(w_ffn_up), bf16(w_ffn_down),
        bf16(w_ple_gate), bf16(w_ple_proj), rows(ln2_g), rows(ln2_b))
    p3 = p.reshape(DEPTH, n, PLE_DIM)

    for layer in range(DEPTH):
        att_h, y_gla, y_lru = _mixer_in(layer, x2, *mixer_params, batch)
        y_att = _attention(att_h, att_diag, batch)
        x2 = _post(layer, y_gla, y_att, y_lru, x2, p3, *post_params)
    return x2.reshape(batch, seq, D_MODEL)
```

```python
import itertools
import math

import numpy as np
import jax
import jax.numpy as jnp
from jax import lax
from jax.experimental import pallas as pl
from jax.experimental.pallas import tpu as pltpu

F32 = jnp.float32
BF16 = jnp.bfloat16

D_MODEL = 1024
CHUNK = 64
GLA_HEADS = 4
GLA_DK = 64
GLA_WIDTH = 256
GLA_GATE_RANK = 16
GLA_GATE_TAU = 16.0
ATT_HEADS = 8
ATT_DH = 64
ATT_WIDTH = 512
ATT_LEFT_CHUNKS = 8
REL_CLIP = 256
LRU_WIDTH = 256
LRU_BLOCKS = 4
LRU_C = 8.0
CONV_WIDTH = 4
D_FF = 2816
PLE_DIM = 256
DEPTH = 2
DN_ALPHA = (2 * DEPTH) ** 0.25
LN_EPS = 1e-5
RMS_EPS = 1e-6
NEG_INF = -1e30
LOG2_E = math.log2(math.e)

LANES = 128
SUBLANES = 8
VMEM_LIMIT_BYTES = 56 * 1024 * 1024

GLA_COLS = 4 * GLA_WIDTH
ATT_COLS = 3 * ATT_WIDTH
LRU_COLS = 2 * LRU_WIDTH
GLR_COLS = LANES
ATT_OFF = GLA_COLS
LRU_OFF = ATT_OFF + ATT_COLS
GLR_OFF = LRU_OFF + LRU_COLS
D_IN_PAD = GLR_OFF + GLR_COLS

ROW_TILE = 1024
MIX_TILE = 512
ATT_SUB = 256
ATT_WIN = ATT_SUB + ATT_LEFT_CHUNKS * CHUNK
WIN_ROWS = ROW_TILE + ATT_LEFT_CHUNKS * CHUNK
ATT_DIAG = ATT_SUB + ATT_WIN
HEADS_PER_PAIR = LANES // ATT_DH
ATT_PAIRS = 4
FF_TILE = 256
PROJ_PIECE = 512
POST_TILE = 1024
POST_ROWS = 256


def _dot(a, b):
    return jnp.dot(a, b, preferred_element_type=F32)


def _dot_nt(a, b):
    return lax.dot_general(a, b, (((1,), (1,)), ((), ())), preferred_element_type=F32)


def _dot_tn(a, b):
    return lax.dot_general(a, b, (((0,), (0,)), ((), ())), preferred_element_type=F32)


def _block_diag_mask(n, blk):
    r = lax.broadcasted_iota(jnp.int32, (n, n), 0) // blk
    c = lax.broadcasted_iota(jnp.int32, (n, n), 1) // blk
    return r == c


def _layer_norm(z, g, b):
    mu = jnp.mean(z, axis=-1, keepdims=True)
    zc = z - mu
    var = jnp.mean(zc * zc, axis=-1, keepdims=True)
    return zc * lax.rsqrt(var + LN_EPS) * g + b


def _softplus(x):
    return jnp.maximum(x, 0.0) + jnp.log1p(jnp.exp(-jnp.abs(x)))


def _shift_rows(x, d, fill, row_idx):
    return jnp.where(row_idx >= d, pltpu.roll(x, d, axis=0), fill)


def _gla_rows(get_h, wup_ref, bup_ref, ng_ref, bdm_ref, o_ref, state_ref, y_ref):
    h, lr = get_h()
    t = h.shape[0]
    z = _dot(lr.astype(BF16), wup_ref[...]) + bup_ref[...]
    la = -_softplus(-z) * (1.0 / GLA_GATE_TAU)
    la_hi = la.astype(BF16)
    la_lo = (la - la_hi.astype(F32)).astype(BF16)
    ri = lax.broadcasted_iota(jnp.int32, (CHUNK, CHUNK), 0)
    ci = lax.broadcasted_iota(jnp.int32, (CHUNK, CHUNK), 1)
    tri = jnp.where(ci <= ri, 1.0, 0.0).astype(BF16)
    bcum, refb, lastb = [], [], []
    for c in range(t // CHUNK):
        rows = slice(c * CHUNK, (c + 1) * CHUNK)
        bc = _dot(tri, la_hi[rows]) + _dot(tri, la_lo[rows])
        bcum.append(bc)
        refb.append(jnp.broadcast_to(bc[CHUNK // 2:CHUNK // 2 + 1], bc.shape))
        lastb.append(jnp.broadcast_to(bc[CHUNK - 1:CHUNK], bc.shape))
    bcum = jnp.concatenate(bcum, axis=0)
    refb = jnp.concatenate(refb, axis=0)
    lastb = jnp.concatenate(lastb, axis=0)
    yield

    q = h[:, 0:GLA_WIDTH] * (GLA_DK ** -0.5)
    k = h[:, GLA_WIDTH:2 * GLA_WIDTH]
    qe = (q * jnp.exp(bcum - refb)).astype(BF16)
    ke = (k * jnp.exp(refb - bcum)).astype(BF16)
    qd = (q * jnp.exp(bcum)).astype(BF16)
    kd = (k * jnp.exp(lastb - bcum)).astype(BF16)
    vb = h[:, 2 * GLA_WIDTH:3 * GLA_WIDTH].astype(BF16)
    yield

    bd = _block_diag_mask(GLA_WIDTH, GLA_DK)
    bd16 = bdm_ref[...] > 0
    zero16 = jnp.zeros((GLA_WIDTH, GLA_WIDTH), BF16)
    ri4 = lax.broadcasted_iota(jnp.int32, (CHUNK, GLA_WIDTH), 0)
    ci4 = lax.broadcasted_iota(jnp.int32, (CHUNK, GLA_WIDTH), 1) % CHUNK
    causal4 = ci4 <= ri4

    state = state_ref[...]
    for c in range(t // CHUNK):
        rows = slice(c * CHUNK, (c + 1) * CHUNK)
        kstack = jnp.where(bd16, jnp.concatenate([ke[rows]] * GLA_HEADS, axis=0), zero16)
        vstack = jnp.where(bd16, jnp.concatenate([vb[rows]] * GLA_HEADS, axis=0), zero16)
        attn = _dot_nt(qe[rows], kstack)
        attn = jnp.where(causal4, attn, 0.0).astype(BF16)
        o_ref[rows, :] = _dot(attn, vstack) + _dot_nt(qd[rows], state.astype(BF16))
        ut = _dot_tn(vb[rows], kd[rows])
        decay = jnp.exp(lastb[c * CHUNK:c * CHUNK + 1, :])
        state = state * decay + jnp.where(bd, ut, 0.0)
        if c % 2 == 1:
            yield
    state_ref[...] = state

    o = o_ref[...]
    group_mean = jnp.where(bd16, jnp.full_like(zero16, 1.0 / GLA_DK), zero16)
    ms = _dot((o * o).astype(BF16), group_mean)
    g = h[:, 3 * GLA_WIDTH:4 * GLA_WIDTH]
    y_ref[...] = (o * lax.rsqrt(ms + RMS_EPS) * ng_ref[...] * (g * jax.nn.sigmoid(g))).astype(BF16)


def _lru_rows(get_h, cw_ref, cb_ref, wa_ref, ba_ref, wi_ref, bi_ref, lam_ref,
              xtail, hcarry, xbuf, a_scr, u_scr, h_scr, hin_scr, y_ref):
    h = get_h()
    t = h.shape[0]
    n_groups = t // SUBLANES
    xbuf[0:SUBLANES, :] = xtail[...]
    xbuf[SUBLANES:SUBLANES + t, :] = h[:, 0:LRU_WIDTH]
    xc = cb_ref[...] + jnp.zeros((t, LRU_WIDTH), F32)
    for w in range(CONV_WIDTH):
        off = SUBLANES - (CONV_WIDTH - 1) + w
        xc = xc + cw_ref[w:w + 1, :] * xbuf[off:off + t, :]
    xtail[...] = xbuf[t:t + SUBLANES, :]
    yield

    xcb = xc.astype(BF16)
    r = jax.nn.sigmoid(_dot(xcb, wa_ref[...]) + ba_ref[...])
    ig = jax.nn.sigmoid(_dot(xcb, wi_ref[...]) + bi_ref[...])
    log_a = (-LRU_C) * r * _softplus(-lam_ref[...])
    a = jnp.exp(log_a)
    th = jnp.tanh(log_a)
    u = jnp.sqrt(-2.0 * th / (1.0 - th)) * (ig * xc)
    yield

    row_in_group = lax.broadcasted_iota(jnp.int32, (t, LRU_WIDTH), 0) % SUBLANES
    d = 1
    while d < SUBLANES:
        u = u + a * _shift_rows(u, d, 0.0, row_in_group)
        a = a * _shift_rows(a, d, 1.0, row_in_group)
        d *= 2
    lane_halves = [slice(v * LANES, (v + 1) * LANES) for v in range(LRU_WIDTH // LANES)]
    group_last = pl.ds(SUBLANES - 1, n_groups, stride=SUBLANES)
    for v, lanes in enumerate(lane_halves):
        a_scr[v] = a[:, lanes]
        u_scr[v] = u[:, lanes]
    yield
    a_g = jnp.concatenate([a_scr[v, group_last, :] for v in range(len(lane_halves))], axis=1)
    u_g = jnp.concatenate([u_scr[v, group_last, :] for v in range(len(lane_halves))], axis=1)
    group_idx = lax.broadcasted_iota(jnp.int32, (n_groups, LRU_WIDTH), 0)
    d = 1
    while d < n_groups:
        u_g = u_g + a_g * _shift_rows(u_g, d, 0.0, group_idx)
        a_g = a_g * _shift_rows(a_g, d, 1.0, group_idx)
        d *= 2
    h_prev = hcarry[0:1, :]
    h_end = u_g + a_g * h_prev
    hin_scr[...] = _shift_rows(h_end, 1, h_prev, group_idx)
    hcarry[...] = jnp.broadcast_to(h_end[n_groups - 1:n_groups, :], hcarry.shape)
    yield
    for g in range(n_groups):
        rows = slice(g * SUBLANES, (g + 1) * SUBLANES)
        for v, lanes in enumerate(lane_halves):
            h_scr[rows, lanes] = u_scr[v, rows, :] + a_scr[v, rows, :] * hin_scr[g:g + 1, lanes]
    y_ref[...] = (h_scr[...] * jax.nn.gelu(h[:, LRU_WIDTH:2 * LRU_WIDTH])).astype(BF16)


def _round_robin(*gens):
    pending = list(gens)
    while pending:
        alive = []
        for g in pending:
            if next(g, StopIteration) is not StopIteration:
                alive.append(g)
                yield
        pending = alive


def _mixer_in_kernel(x_ref, wt_ref, wup_ref, bup_ref, ng_ref, bdm_ref,
                     cw_ref, cb_ref, wa_ref, ba_ref, wi_ref, bi_ref, lam_ref,
                     att_ref, yg_ref, yl_ref,
                     w_ref, state_scr, xtail, hcarry, o_scr, xbuf, a_scr, u_scr, h_scr, hin_scr):
    @pl.when((pl.program_id(0) == 0) & (pl.program_id(1) == 0))
    def _():
        for c in range(0, D_IN_PAD, LANES):
            w_ref[:, c:c + LANES] = wt_ref[c:c + LANES, :].T

    @pl.when(pl.program_id(1) == 0)
    def _():
        state_scr[...] = jnp.zeros_like(state_scr)
        xtail[...] = jnp.zeros_like(xtail)
        hcarry[...] = jnp.zeros_like(hcarry)

    xb = x_ref[...].astype(BF16)
    out = {}

    def project(name, off, width):
        cols = []
        for c in range(0, width, PROJ_PIECE):
            n_cols = min(PROJ_PIECE, width - c)
            cols.append(_dot(xb, w_ref[:, off + c:off + c + n_cols]))
            if c + n_cols == width:
                out[name] = jnp.concatenate(cols, axis=1) if len(cols) > 1 else cols[0]
            yield

    def project_attention():
        for c in range(0, ATT_COLS, PROJ_PIECE):
            att_ref[:, c:c + PROJ_PIECE] = _dot(
                xb, w_ref[:, ATT_OFF + c:ATT_OFF + c + PROJ_PIECE]).astype(BF16)
            yield

    lru = _lru_rows(lambda: out["lru"], cw_ref, cb_ref, wa_ref, ba_ref, wi_ref, bi_ref, lam_ref,
                    xtail, hcarry, xbuf, a_scr, u_scr, h_scr, hin_scr, yl_ref)
    gla = _gla_rows(lambda: (out["gla"], out["glr"]), wup_ref, bup_ref, ng_ref, bdm_ref,
                    o_scr, state_scr, yg_ref)
    stages = (project("lru", LRU_OFF, LRU_COLS),
              _round_robin(itertools.chain(project("glr", GLR_OFF, GLR_COLS),
                                           project("gla", 0, GLA_COLS)), lru),
              _round_robin(project_attention(), gla))
    for _ in itertools.chain(*stages):
        pass


def _mixer_in(layer, x2, w_perm, wup_pad, bup, ng_row, conv_w, conv_b, wa_bd, b_a, wi_bd, b_i, lam,
              batch):
    n = x2.shape[0]
    nt = n // batch // MIX_TILE
    d = np.arange(GLA_WIDTH) // GLA_DK
    bdm = jnp.asarray((d[:, None] == d[None, :]).astype(np.float32), BF16)
    row = lambda c: pl.BlockSpec((MIX_TILE, c), lambda b, i: (b * nt + i, 0))
    full = lambda r, c: pl.BlockSpec((r, c), lambda b, i: (0, 0))
    of_layer = lambda r, c: pl.BlockSpec((None, r, c), lambda b, i: (layer, 0, 0))
    tile = lambda r, c: pltpu.VMEM((r, c), F32)
    lane_halves = pltpu.VMEM((LRU_WIDTH // LANES, MIX_TILE, LANES), F32)
    return pl.pallas_call(
        _mixer_in_kernel,
        grid=(batch, nt),
        in_specs=[row(D_MODEL),
                  pl.BlockSpec((None, D_IN_PAD, D_MODEL), lambda b, i: (layer, 0, 0),
                               pipeline_mode=pl.Buffered(1)),
                  of_layer(GLR_COLS, GLA_WIDTH), of_layer(1, GLA_WIDTH), of_layer(1, GLA_WIDTH),
                  full(GLA_WIDTH, GLA_WIDTH),
                  of_layer(CONV_WIDTH, LRU_WIDTH), of_layer(1, LRU_WIDTH),
                  of_layer(LRU_WIDTH, LRU_WIDTH), of_layer(1, LRU_WIDTH),
                  of_layer(LRU_WIDTH, LRU_WIDTH), of_layer(1, LRU_WIDTH), of_layer(1, LRU_WIDTH)],
        out_specs=[row(ATT_COLS), row(GLA_WIDTH), row(LRU_WIDTH)],
        out_shape=[jax.ShapeDtypeStruct((n, ATT_COLS), BF16),
                   jax.ShapeDtypeStruct((n, GLA_WIDTH), BF16),
                   jax.ShapeDtypeStruct((n, LRU_WIDTH), BF16)],
        scratch_shapes=[pltpu.VMEM((D_MODEL, D_IN_PAD), BF16),
                        pltpu.VMEM((GLA_WIDTH, GLA_WIDTH), F32),
                        pltpu.VMEM((SUBLANES, LRU_WIDTH), F32),
                        pltpu.VMEM((SUBLANES, LRU_WIDTH), F32),
                        tile(MIX_TILE, GLA_WIDTH),
                        tile(MIX_TILE + SUBLANES, LRU_WIDTH),
                        lane_halves, lane_halves,
                        tile(MIX_TILE, LRU_WIDTH),
                        tile(MIX_TILE // SUBLANES, LRU_WIDTH)],
        compiler_params=pltpu.CompilerParams(
            dimension_semantics=("arbitrary", "arbitrary"), vmem_limit_bytes=VMEM_LIMIT_BYTES),
        name="mixer_in",
    )(x2, w_perm, wup_pad, bup, ng_row, bdm, conv_w, conv_b, wa_bd, b_a, wi_bd, b_i, lam)


def _att_kernel(q_ref, kw_ref, vw_ref, diag_ref, hm_ref, y_ref, bias_ref):
    lane_head0 = lax.broadcasted_iota(jnp.int32, (ATT_SUB, LANES), 1) < ATT_DH

    @pl.when((pl.program_id(1) == 0) & (pl.program_id(2) == 0))
    def _():
        r = lax.broadcasted_iota(jnp.int32, (ATT_SUB, ATT_WIN), 0)
        w = lax.broadcasted_iota(jnp.int32, (ATT_SUB, ATT_WIN), 1)
        kk = w - (r // CHUNK) * CHUNK
        in_band = (kk >= 0) & (kk < (ATT_LEFT_CHUNKS + 1) * CHUNK)
        for hh in range(ATT_PAIRS * HEADS_PER_PAIR):
            rows = jnp.broadcast_to(diag_ref[hh:hh + 1, :], (ATT_SUB, ATT_DIAG))
            skew = pltpu.roll(rows, ATT_DIAG - (ATT_SUB - 1), axis=1, stride=1, stride_axis=0)
            bias_ref[hh] = jnp.where(in_band, skew[:, 0:ATT_WIN], NEG_INF)

    def body(first):
        n_keys = ROW_TILE if first else WIN_ROWS
        head0 = hm_ref[0:n_keys, :] > 0
        vh, qh = [], []
        for pair in range(ATT_PAIRS):
            lanes = slice(pair * LANES, (pair + 1) * LANES)
            v = vw_ref[0:n_keys, lanes]
            one16 = jnp.ones_like(v)
            vh.append((jnp.where(head0, v, one16), jnp.where(head0, one16, v)))
            q = q_ref[:, lanes]
            zero16 = jnp.zeros_like(q)
            qh.append((jnp.where(head0[0:ROW_TILE], q, zero16),
                       jnp.where(head0[0:ROW_TILE], zero16, q)))

        def window(j):
            if first:
                return slice(max(0, (j + 1) * ATT_SUB - ATT_WIN), (j + 1) * ATT_SUB)
            return slice(j * ATT_SUB, j * ATT_SUB + ATT_WIN)

        def scores(pair, j, hh):
            win = window(j)
            n_win = win.stop - win.start
            rows = slice(j * ATT_SUB, (j + 1) * ATT_SUB)
            return (_dot_nt(qh[pair][hh][rows], kw_ref[win, pair * LANES:(pair + 1) * LANES])
                    + bias_ref[pair * HEADS_PER_PAIR + hh, :, ATT_WIN - n_win:ATT_WIN])

        units = [(pair, j, hh) for pair in range(ATT_PAIRS)
                 for j in range(ROW_TILE // ATT_SUB) for hh in range(HEADS_PER_PAIR)]
        n_units = len(units)
        s, p, outs = {}, {}, []
        for step in range(n_units + 2):
            if step < n_units:
                s[step] = scores(*units[step])
            if step >= 2:
                pair, j, hh = units[step - 2]
                of = _dot(p.pop(step - 2), vh[pair][hh][window(j)])
                outs.append(of / pltpu.roll(of, ATT_DH, axis=1))
                if hh == HEADS_PER_PAIR - 1:
                    rows = slice(j * ATT_SUB, (j + 1) * ATT_SUB)
                    y_ref[rows, pair * LANES:(pair + 1) * LANES] = jnp.where(
                        lane_head0, outs[0], outs[1]).astype(BF16)
                    outs = []
            if 1 <= step <= n_units:
                su = s.pop(step - 1)
                m = jnp.max(su, axis=-1, keepdims=True)
                p[step - 1] = jnp.exp2(su - m).astype(BF16)

    @pl.when(pl.program_id(2) == 0)
    def _():
        body(True)

    @pl.when(pl.program_id(2) > 0)
    def _():
        body(False)


def _attention(att_h, diag, batch):
    n = att_h.shape[0]
    seq = n // batch
    att3 = att_h.reshape(batch, seq, ATT_COLS)
    groups = ATT_WIDTH // (ATT_PAIRS * LANES)
    width = ATT_PAIRS * LANES
    head0 = jnp.asarray(
        np.broadcast_to(np.arange(LANES) < ATT_DH, (WIN_ROWS, LANES)).astype(np.float32), BF16)
    left = ATT_LEFT_CHUNKS * CHUNK
    window = lambda off: pl.BlockSpec(
        (None, pl.Element(WIN_ROWS), pl.Element(width)),
        lambda g, b, i: (b, jnp.maximum(i * (ROW_TILE // left) - 1, 0) * left,
                         (off + g * ATT_PAIRS) * LANES))
    y = pl.pallas_call(
        _att_kernel,
        grid=(groups, batch, seq // ROW_TILE),
        in_specs=[pl.BlockSpec((None, ROW_TILE, width), lambda g, b, i: (b, i, g)),
                  window(ATT_WIDTH // LANES), window(2 * ATT_WIDTH // LANES),
                  pl.BlockSpec((None, ATT_PAIRS * HEADS_PER_PAIR, ATT_DIAG),
                               lambda g, b, i: (g, 0, 0)),
                  pl.BlockSpec((WIN_ROWS, LANES), lambda g, b, i: (0, 0))],
        out_specs=pl.BlockSpec((None, ROW_TILE, width), lambda g, b, i: (b, i, g)),
        out_shape=jax.ShapeDtypeStruct((batch, seq, ATT_WIDTH), BF16),
        scratch_shapes=[pltpu.VMEM((ATT_PAIRS * HEADS_PER_PAIR, ATT_SUB, ATT_WIN), F32)],
        compiler_params=pltpu.CompilerParams(
            dimension_semantics=("arbitrary", "arbitrary", "arbitrary"),
            vmem_limit_bytes=VMEM_LIMIT_BYTES),
        name="band_attention",
    )(att3, att3, att3, diag, head0)
    return y.reshape(n, ATT_WIDTH)


def _attention_diag(rel_bias):
    n_diag = ATT_SUB + ATT_WIN - 1
    rel = ATT_LEFT_CHUNKS * CHUNK + (ATT_SUB - 1) - np.arange(n_diag)
    idx = np.clip(rel, -REL_CLIP, REL_CLIP) + REL_CLIP
    diag = jnp.pad(rel_bias.astype(F32)[:, idx] * LOG2_E, ((0, 0), (0, ATT_DIAG - n_diag)))
    return diag.reshape(-1, ATT_PAIRS * HEADS_PER_PAIR, ATT_DIAG)


def _post_kernel(yg_ref, ya_ref, yl_ref, x_ref, p_ref, wo_ref, g1_ref, b1_ref,
                 wg_ref, wu_ref, wd_ref, wpg_ref, wpp_ref, g2_ref, b2_ref, o_ref, act_ref):
    n_parts = POST_TILE // POST_ROWS
    rows = [slice(k * POST_ROWS, (k + 1) * POST_ROWS) for k in range(n_parts)]
    n_ff = D_FF // FF_TILE
    mix, x1, xb, ffn, ple = ({} for _ in range(5))

    def out_projection(k):
        mix[k] = (_dot(yg_ref[rows[k], :], wo_ref[0:GLA_WIDTH, :])
                  + _dot(ya_ref[rows[k], :], wo_ref[GLA_WIDTH:GLA_WIDTH + ATT_WIDTH, :])
                  + _dot(yl_ref[rows[k], :], wo_ref[GLA_WIDTH + ATT_WIDTH:D_MODEL, :]))

    def norm1(k):
        x1[k] = _layer_norm(DN_ALPHA * x_ref[rows[k], :] + mix.pop(k), g1_ref[...], b1_ref[...])
        xb[k] = x1[k].astype(BF16)

    def ffn_up(k, c):
        cols = slice(c * FF_TILE, (c + 1) * FF_TILE)
        gate = _dot(xb[k], wg_ref[:, cols])
        up = _dot(xb[k], wu_ref[:, cols])
        act_ref[k % 2, :, cols] = (gate * jax.nn.sigmoid(gate) * up).astype(BF16)

    def ffn_down(k):
        ffn[k] = _dot(act_ref[k % 2], wd_ref[...])
        ple[k] = (jax.nn.sigmoid(_dot(xb.pop(k), wpg_ref[...]))
                  * _dot(p_ref[rows[k], :].astype(BF16), wpp_ref[...]))

    def norm2(k):
        o_ref[rows[k], :] = _layer_norm(DN_ALPHA * x1.pop(k) + ffn.pop(k) + ple.pop(k),
                                        g2_ref[...], b2_ref[...])

    out_projection(0)
    norm1(0)
    for k in range(n_parts + 2):
        if k < n_parts:
            if k + 1 < n_parts:
                out_projection(k + 1)
            for c in range(n_ff):
                ffn_up(k, c)
                if c == n_ff // 2 and k + 1 < n_parts:
                    norm1(k + 1)
        if 1 <= k <= n_parts:
            ffn_down(k - 1)
        if k >= 2:
            norm2(k - 2)


def _post(layer, y_gla, y_att, y_lru, x2, p3, wo, g1, b1, wg, wu, wd, wpg, wpp, g2, b2):
    n = x2.shape[0]
    row = lambda c: pl.BlockSpec((POST_TILE, c), lambda i: (i, 0))
    full = lambda r, c: pl.BlockSpec((None, r, c), lambda i: (layer, 0, 0),
                                     pipeline_mode=pl.Buffered(1))
    return pl.pallas_call(
        _post_kernel,
        grid=(n // POST_TILE,),
        in_specs=[row(GLA_WIDTH), row(ATT_WIDTH), row(LRU_WIDTH), row(D_MODEL),
                  pl.BlockSpec((None, POST_TILE, PLE_DIM), lambda i: (layer, i, 0)),
                  full(D_MODEL, D_MODEL), full(1, D_MODEL), full(1, D_MODEL),
                  full(D_MODEL, D_FF), full(D_MODEL, D_FF), full(D_FF, D_MODEL),
                  full(D_MODEL, D_MODEL), full(PLE_DIM, D_MODEL),
                  full(1, D_MODEL), full(1, D_MODEL)],
        out_specs=row(D_MODEL),
        out_shape=jax.ShapeDtypeStruct((n, D_MODEL), F32),
        scratch_shapes=[pltpu.VMEM((2, POST_ROWS, D_FF), BF16)],
        compiler_params=pltpu.CompilerParams(
            dimension_semantics=("arbitrary",), vmem_limit_bytes=VMEM_LIMIT_BYTES),
        name="out_proj_ffn",
    )(y_gla, y_att, y_lru, x2, p3, wo, g1, b1, wg, wu, wd, wpg, wpp, g2, b2)


def _block_diag(w):
    n_layers, g, c, d = w.shape
    eye = jnp.eye(g, dtype=w.dtype)
    return (eye[None, :, None, :, None] * w[:, :, :, None, :]).reshape(n_layers, g * c, g * d)


def _permute_w_in(w):
    wt = jnp.swapaxes(w, 1, 2)
    glr_lo = GLA_COLS
    glr_hi = GLA_COLS + GLA_GATE_RANK
    aq_hi = glr_hi + ATT_WIDTH
    pad = jnp.zeros((wt.shape[0], GLR_COLS - GLA_GATE_RANK, wt.shape[2]), w.dtype)
    return jnp.concatenate(
        [wt[:, :glr_lo], wt[:, glr_hi:aq_hi] * (ATT_DH ** -0.5 * LOG2_E), wt[:, aq_hi:],
         wt[:, glr_lo:glr_hi], pad], axis=1).astype(BF16)


def kernel(x, p, w_in, gla_w_lr_up, gla_b_lr_up, gla_norm_g, rel_bias, lru_conv_w, lru_conv_b,
           lru_w_a, lru_b_a, lru_w_i, lru_b_i, lru_lambda, w_out, ln1_g, ln1_b, w_ffn_gate,
           w_ffn_up, w_ffn_down, w_ple_gate, w_ple_proj, ln2_g, ln2_b):
    batch, seq, _ = x.shape
    n = batch * seq
    assert seq % ROW_TILE == 0
    x2 = x.reshape(n, D_MODEL)
    att_diag = _attention_diag(rel_bias)
    rows = lambda a: a.reshape(DEPTH, 1, -1).astype(F32)
    bf16 = lambda a: a.astype(BF16)

    mixer_params = (
        _permute_w_in(w_in),
        bf16(jnp.pad(gla_w_lr_up, ((0, 0), (0, GLR_COLS - GLA_GATE_RANK), (0, 0)))),
        rows(gla_b_lr_up), rows(jnp.tile(gla_norm_g, (1, GLA_HEADS))),
        lru_conv_w.astype(F32), rows(lru_conv_b),
        bf16(_block_diag(lru_w_a)), rows(lru_b_a),
        bf16(_block_diag(lru_w_i)), rows(lru_b_i), rows(lru_lambda))
    post_params = (
        bf16(w_out), rows(ln1_g), rows(ln1_b), bf16(w_ffn_gate), bf16(w_ffn_up), bf16(w_ffn_down),
        bf16(w_ple_gate), bf16(w_ple_proj), rows(ln2_g), rows(ln2_b))
    p3 = p.reshape(DEPTH, n, PLE_DIM)

    for layer in range(DEPTH):
        att_h, y_gla, y_lru = _mixer_in(layer, x2, *mixer_params, batch)
        y_att = _attention(att_h, att_diag, batch)
        x2 = _post(layer, y_gla, y_att, y_lru, x2, p3, *post_params)
    return x2.reshape(batch, seq, D_MODEL)
```

```python
import itertools
import math

import numpy as np
import jax
import jax.numpy as jnp
from jax import lax
from jax.experimental import pallas as pl
from jax.experimental.pallas import tpu as pltpu

F32 = jnp.float32
BF16 = jnp.bfloat16

D_MODEL = 1024
CHUNK = 64
GLA_HEADS = 4
GLA_DK = 64
GLA_WIDTH = 256
GLA_GATE_RANK = 16
GLA_GATE_TAU = 16.0
ATT_HEADS = 8
ATT_DH = 64
ATT_WIDTH = 512
ATT_LEFT_CHUNKS = 8
REL_CLIP = 256
LRU_WIDTH = 256
LRU_BLOCKS = 4
LRU_C = 8.0
CONV_WIDTH = 4
D_FF = 2816
PLE_DIM = 256
DEPTH = 2
DN_ALPHA = (2 * DEPTH) ** 0.25
LN_EPS = 1e-5
RMS_EPS = 1e-6
NEG_INF = -1e30
LOG2_E = math.log2(math.e)

LANES = 128
SUBLANES = 8
VMEM_LIMIT_BYTES = 56 * 1024 * 1024

GLA_COLS = 4 * GLA_WIDTH
ATT_COLS = 3 * ATT_WIDTH
LRU_COLS = 2 * LRU_WIDTH
GLR_COLS = LANES
ATT_OFF = GLA_COLS
LRU_OFF = ATT_OFF + ATT_COLS
GLR_OFF = LRU_OFF + LRU_COLS
D_IN_PAD = GLR_OFF + GLR_COLS

ROW_TILE = 1024
MIX_TILE = 512
ATT_SUB = 256
ATT_WIN = ATT_SUB + ATT_LEFT_CHUNKS * CHUNK
WIN_ROWS = ROW_TILE + ATT_LEFT_CHUNKS * CHUNK
ATT_DIAG = ATT_SUB + ATT_WIN
HEADS_PER_PAIR = LANES // ATT_DH
ATT_PAIRS = 4
FF_TILE = 256
PROJ_PIECE = 512
POST_TILE = 1024
POST_ROWS = 256


def _dot(a, b):
    return jnp.dot(a, b, preferred_element_type=F32)


def _dot_nt(a, b):
    return lax.dot_general(a, b, (((1,), (1,)), ((), ())), preferred_element_type=F32)


def _dot_tn(a, b):
    return lax.dot_general(a, b, (((0,), (0,)), ((), ())), preferred_element_type=F32)


def _block_diag_mask(n, blk):
    r = lax.broadcasted_iota(jnp.int32, (n, n), 0) // blk
    c = lax.broadcasted_iota(jnp.int32, (n, n), 1) // blk
    return r == c


def _layer_norm(z, g, b):
    mu = jnp.mean(z, axis=-1, keepdims=True)
    zc = z - mu
    var = jnp.mean(zc * zc, axis=-1, keepdims=True)
    return zc * lax.rsqrt(var + LN_EPS) * g + b


def _softplus(x):
    return jnp.maximum(x, 0.0) + jnp.log1p(jnp.exp(-jnp.abs(x)))


class _Rows:
    def __init__(self, ref, lo, hi):
        self.ref, self.lo, self.hi = ref, lo, hi

    def __getitem__(self, idx):
        if idx is Ellipsis:
            return self.ref[self.lo:self.hi, :]
        rows, cols = idx
        return self.ref[self.lo + rows.start:self.lo + rows.stop, cols]


def _shift_rows(x, d, fill, row_idx):
    return jnp.where(row_idx >= d, pltpu.roll(x, d, axis=0), fill)


def _gla_rows(get_h, wup_ref, bup_ref, ng_ref, bdm_ref, o_ref, state_ref, y_ref):
    h, lr = get_h()
    t = h.shape[0]
    z = _dot(lr.astype(BF16), wup_ref[...]) + bup_ref[...]
    la = -_softplus(-z) * (1.0 / GLA_GATE_TAU)
    la_hi = la.astype(BF16)
    la_lo = (la - la_hi.astype(F32)).astype(BF16)
    ri = lax.broadcasted_iota(jnp.int32, (CHUNK, CHUNK), 0)
    ci = lax.broadcasted_iota(jnp.int32, (CHUNK, CHUNK), 1)
    tri = jnp.where(ci <= ri, 1.0, 0.0).astype(BF16)
    bcum, refb, lastb = [], [], []
    for c in range(t // CHUNK):
        rows = slice(c * CHUNK, (c + 1) * CHUNK)
        bc = _dot(tri, la_hi[rows]) + _dot(tri, la_lo[rows])
        bcum.append(bc)
        refb.append(jnp.broadcast_to(bc[CHUNK // 2:CHUNK // 2 + 1], bc.shape))
        lastb.append(jnp.broadcast_to(bc[CHUNK - 1:CHUNK], bc.shape))
    bcum = jnp.concatenate(bcum, axis=0)
    refb = jnp.concatenate(refb, axis=0)
    lastb = jnp.concatenate(lastb, axis=0)
    yield

    q = h[:, 0:GLA_WIDTH] * (GLA_DK ** -0.5)
    k = h[:, GLA_WIDTH:2 * GLA_WIDTH]
    qe = (q * jnp.exp(bcum - refb)).astype(BF16)
    ke = (k * jnp.exp(refb - bcum)).astype(BF16)
    qd = (q * jnp.exp(bcum)).astype(BF16)
    kd = (k * jnp.exp(lastb - bcum)).astype(BF16)
    vb = h[:, 2 * GLA_WIDTH:3 * GLA_WIDTH].astype(BF16)
    yield

    bd = _block_diag_mask(GLA_WIDTH, GLA_DK)
    bd16 = bdm_ref[...] > 0
    zero16 = jnp.zeros((GLA_WIDTH, GLA_WIDTH), BF16)
    ri4 = lax.broadcasted_iota(jnp.int32, (CHUNK, GLA_WIDTH), 0)
    ci4 = lax.broadcasted_iota(jnp.int32, (CHUNK, GLA_WIDTH), 1) % CHUNK
    causal4 = ci4 <= ri4

    state = state_ref[...]
    for c in range(t // CHUNK):
        rows = slice(c * CHUNK, (c + 1) * CHUNK)
        kstack = jnp.where(bd16, jnp.concatenate([ke[rows]] * GLA_HEADS, axis=0), zero16)
        vstack = jnp.where(bd16, jnp.concatenate([vb[rows]] * GLA_HEADS, axis=0), zero16)
        attn = _dot_nt(qe[rows], kstack)
        attn = jnp.where(causal4, attn, 0.0).astype(BF16)
        o_ref[rows, :] = _dot(attn, vstack) + _dot_nt(qd[rows], state.astype(BF16))
        ut = _dot_tn(vb[rows], kd[rows])
        decay = jnp.exp(lastb[c * CHUNK:c * CHUNK + 1, :])
        state = state * decay + jnp.where(bd, ut, 0.0)
        if c % 2 == 1:
            yield
    state_ref[...] = state

    o = o_ref[...]
    group_mean = jnp.where(bd16, jnp.full_like(zero16, 1.0 / GLA_DK), zero16)
    ms = _dot((o * o).astype(BF16), group_mean)
    g = h[:, 3 * GLA_WIDTH:4 * GLA_WIDTH]
    y_ref[...] = (o * lax.rsqrt(ms + RMS_EPS) * ng_ref[...] * (g * jax.nn.sigmoid(g))).astype(BF16)


def _lru_rows(get_h, cw_ref, cb_ref, wa_ref, ba_ref, wi_ref, bi_ref, lam_ref,
              xtail, hcarry, xbuf, a_scr, u_scr, h_scr, hin_scr, y_ref):
    h = get_h()
    t = h.shape[0]
    n_groups = t // SUBLANES
    xbuf[0:SUBLANES, :] = xtail[...]
    xbuf[SUBLANES:SUBLANES + t, :] = h[:, 0:LRU_WIDTH]
    xc = cb_ref[...] + jnp.zeros((t, LRU_WIDTH), F32)
    for w in range(CONV_WIDTH):
        off = SUBLANES - (CONV_WIDTH - 1) + w
        xc = xc + cw_ref[w:w + 1, :] * xbuf[off:off + t, :]
    xtail[...] = xbuf[t:t + SUBLANES, :]
    yield

    xcb = xc.astype(BF16)
    r = jax.nn.sigmoid(_dot(xcb, wa_ref[...]) + ba_ref[...])
    ig = jax.nn.sigmoid(_dot(xcb, wi_ref[...]) + bi_ref[...])
    log_a = (-LRU_C) * r * _softplus(-lam_ref[...])
    a = jnp.exp(log_a)
    th = jnp.tanh(log_a)
    u = jnp.sqrt(-2.0 * th / (1.0 - th)) * (ig * xc)
    yield

    row_in_group = lax.broadcasted_iota(jnp.int32, (t, LRU_WIDTH), 0) % SUBLANES
    d = 1
    while d < SUBLANES:
        u = u + a * _shift_rows(u, d, 0.0, row_in_group)
        a = a * _shift_rows(a, d, 1.0, row_in_group)
        d *= 2
    lane_halves = [slice(v * LANES, (v + 1) * LANES) for v in range(LRU_WIDTH // LANES)]
    group_last = pl.ds(SUBLANES - 1, n_groups, stride=SUBLANES)
    for v, lanes in enumerate(lane_halves):
        a_scr[v] = a[:, lanes]
        u_scr[v] = u[:, lanes]
    yield
    a_g = jnp.concatenate([a_scr[v, group_last, :] for v in range(len(lane_halves))], axis=1)
    u_g = jnp.concatenate([u_scr[v, group_last, :] for v in range(len(lane_halves))], axis=1)
    group_idx = lax.broadcasted_iota(jnp.int32, (n_groups, LRU_WIDTH), 0)
    d = 1
    while d < n_groups:
        u_g = u_g + a_g * _shift_rows(u_g, d, 0.0, group_idx)
        a_g = a_g * _shift_rows(a_g, d, 1.0, group_idx)
        d *= 2
    h_prev = hcarry[0:1, :]
    h_end = u_g + a_g * h_prev
    hin_scr[...] = _shift_rows(h_end, 1, h_prev, group_idx)
    hcarry[...] = jnp.broadcast_to(h_end[n_groups - 1:n_groups, :], hcarry.shape)
    yield
    for g in range(n_groups):
        rows = slice(g * SUBLANES, (g + 1) * SUBLANES)
        for v, lanes in enumerate(lane_halves):
            h_scr[rows, lanes] = u_scr[v, rows, :] + a_scr[v, rows, :] * hin_scr[g:g + 1, lanes]
    y_ref[...] = (h_scr[...] * jax.nn.gelu(h[:, LRU_WIDTH:2 * LRU_WIDTH])).astype(BF16)


def _round_robin(*gens):
    pending = list(gens)
    while pending:
        alive = []
        for g in pending:
            if next(g, StopIteration) is not StopIteration:
                alive.append(g)
                yield
        pending = alive


def _mixer_in_kernel(x_ref, wt_ref, wup_ref, vec_ref, bdm_ref, wa_ref, wi_ref,
                     att_ref, yg_ref, yl_ref,
                     w_ref, state_scr, xtail, hcarry, o_scr, xbuf, a_scr, u_scr, h_scr, hin_scr):
    @pl.when((pl.program_id(0) == 0) & (pl.program_id(1) == 0))
    def _():
        for c in range(0, D_IN_PAD, LANES):
            w_ref[:, c:c + LANES] = wt_ref[c:c + LANES, :].T

    @pl.when(pl.program_id(1) == 0)
    def _():
        state_scr[...] = jnp.zeros_like(state_scr)
        xtail[...] = jnp.zeros_like(xtail)
        hcarry[...] = jnp.zeros_like(hcarry)

    bup_ref, ng_ref, cb_ref, ba_ref, bi_ref, lam_ref = (_Rows(vec_ref, r, r + 1) for r in range(6))
    cw_ref = _Rows(vec_ref, 6, 6 + CONV_WIDTH)

    xb = x_ref[...].astype(BF16)
    out = {}

    def project(name, off, width):
        cols = []
        for c in range(0, width, PROJ_PIECE):
            n_cols = min(PROJ_PIECE, width - c)
            cols.append(_dot(xb, w_ref[:, off + c:off + c + n_cols]))
            if c + n_cols == width:
                out[name] = jnp.concatenate(cols, axis=1) if len(cols) > 1 else cols[0]
            yield

    def project_attention():
        for c in range(0, ATT_COLS, PROJ_PIECE):
            att_ref[:, c:c + PROJ_PIECE] = _dot(
                xb, w_ref[:, ATT_OFF + c:ATT_OFF + c + PROJ_PIECE]).astype(BF16)
            yield

    lru = _lru_rows(lambda: out["lru"], cw_ref, cb_ref, wa_ref, ba_ref, wi_ref, bi_ref, lam_ref,
                    xtail, hcarry, xbuf, a_scr, u_scr, h_scr, hin_scr, yl_ref)
    gla = _gla_rows(lambda: (out["gla"], out["glr"]), wup_ref, bup_ref, ng_ref, bdm_ref,
                    o_scr, state_scr, yg_ref)
    stages = (project("lru", LRU_OFF, LRU_COLS),
              _round_robin(itertools.chain(project("glr", GLR_OFF, GLR_COLS),
                                           project("gla", 0, GLA_COLS)), lru),
              _round_robin(project_attention(), gla))
    for _ in itertools.chain(*stages):
        pass


def _mixer_vectors(gla_b_lr_up, gla_norm_g, lru_conv_b, lru_b_a, lru_b_i, lru_lambda, lru_conv_w):
    rows = jnp.stack([gla_b_lr_up, jnp.tile(gla_norm_g, (1, GLA_HEADS)), lru_conv_b, lru_b_a,
                      lru_b_i, lru_lambda], axis=1)
    return jnp.concatenate([rows, lru_conv_w], axis=1).astype(F32)


def _mixer_in(layer, x2, w_perm, wup_pad, vectors, wa_bd, wi_bd, batch):
    n = x2.shape[0]
    nt = n // batch // MIX_TILE
    d = np.arange(GLA_WIDTH) // GLA_DK
    bdm = jnp.asarray((d[:, None] == d[None, :]).astype(np.float32), BF16)
    row = lambda c: pl.BlockSpec((MIX_TILE, c), lambda b, i: (b * nt + i, 0))
    full = lambda r, c: pl.BlockSpec((r, c), lambda b, i: (0, 0))
    of_layer = lambda r, c: pl.BlockSpec((None, r, c), lambda b, i: (layer, 0, 0))
    tile = lambda r, c: pltpu.VMEM((r, c), F32)
    lane_halves = pltpu.VMEM((LRU_WIDTH // LANES, MIX_TILE, LANES), F32)
    return pl.pallas_call(
        _mixer_in_kernel,
        grid=(batch, nt),
        in_specs=[row(D_MODEL),
                  pl.BlockSpec((None, D_IN_PAD, D_MODEL), lambda b, i: (layer, 0, 0),
                               pipeline_mode=pl.Buffered(1)),
                  of_layer(GLR_COLS, GLA_WIDTH), of_layer(6 + CONV_WIDTH, GLA_WIDTH),
                  full(GLA_WIDTH, GLA_WIDTH),
                  of_layer(LRU_WIDTH, LRU_WIDTH), of_layer(LRU_WIDTH, LRU_WIDTH)],
        out_specs=[row(ATT_COLS), row(GLA_WIDTH), row(LRU_WIDTH)],
        out_shape=[jax.ShapeDtypeStruct((n, ATT_COLS), BF16),
                   jax.ShapeDtypeStruct((n, GLA_WIDTH), BF16),
                   jax.ShapeDtypeStruct((n, LRU_WIDTH), BF16)],
        scratch_shapes=[pltpu.VMEM((D_MODEL, D_IN_PAD), BF16),
                        pltpu.VMEM((GLA_WIDTH, GLA_WIDTH), F32),
                        pltpu.VMEM((SUBLANES, LRU_WIDTH), F32),
                        pltpu.VMEM((SUBLANES, LRU_WIDTH), F32),
                        tile(MIX_TILE, GLA_WIDTH),
                        tile(MIX_TILE + SUBLANES, LRU_WIDTH),
                        lane_halves, lane_halves,
                        tile(MIX_TILE, LRU_WIDTH),
                        tile(MIX_TILE // SUBLANES, LRU_WIDTH)],
        compiler_params=pltpu.CompilerParams(
            dimension_semantics=("arbitrary", "arbitrary"), vmem_limit_bytes=VMEM_LIMIT_BYTES),
        name="mixer_in",
    )(x2, w_perm, wup_pad, vectors, bdm, wa_bd, wi_bd)


def _att_kernel(q_ref, kw_ref, vw_ref, diag_ref, hm_ref, y_ref, bias_ref):
    lane_head0 = lax.broadcasted_iota(jnp.int32, (ATT_SUB, LANES), 1) < ATT_DH

    @pl.when((pl.program_id(1) == 0) & (pl.program_id(2) == 0))
    def _():
        r = lax.broadcasted_iota(jnp.int32, (ATT_SUB, ATT_WIN), 0)
        w = lax.broadcasted_iota(jnp.int32, (ATT_SUB, ATT_WIN), 1)
        kk = w - (r // CHUNK) * CHUNK
        in_band = (kk >= 0) & (kk < (ATT_LEFT_CHUNKS + 1) * CHUNK)
        for hh in range(ATT_PAIRS * HEADS_PER_PAIR):
            rows = jnp.broadcast_to(diag_ref[hh:hh + 1, :], (ATT_SUB, ATT_DIAG))
            skew = pltpu.roll(rows, ATT_DIAG - (ATT_SUB - 1), axis=1, stride=1, stride_axis=0)
            bias_ref[hh] = jnp.where(in_band, skew[:, 0:ATT_WIN], NEG_INF)

    def body(first):
        n_keys = ROW_TILE if first else WIN_ROWS
        head0 = hm_ref[0:n_keys, :] > 0
        vh, qh = [], []
        for pair in range(ATT_PAIRS):
            lanes = slice(pair * LANES, (pair + 1) * LANES)
            v = vw_ref[0:n_keys, lanes]
            one16 = jnp.ones_like(v)
            vh.append((jnp.where(head0, v, one16), jnp.where(head0, one16, v)))
            q = q_ref[:, lanes]
            zero16 = jnp.zeros_like(q)
            qh.append((jnp.where(head0[0:ROW_TILE], q, zero16),
                       jnp.where(head0[0:ROW_TILE], zero16, q)))

        def window(j):
            if first:
                return slice(max(0, (j + 1) * ATT_SUB - ATT_WIN), (j + 1) * ATT_SUB)
            return slice(j * ATT_SUB, j * ATT_SUB + ATT_WIN)

        def scores(pair, j, hh):
            win = window(j)
            n_win = win.stop - win.start
            rows = slice(j * ATT_SUB, (j + 1) * ATT_SUB)
            return (_dot_nt(qh[pair][hh][rows], kw_ref[win, pair * LANES:(pair + 1) * LANES])
                    + bias_ref[pair * HEADS_PER_PAIR + hh, :, ATT_WIN - n_win:ATT_WIN])

        units = [(pair, j, hh) for pair in range(ATT_PAIRS)
                 for j in range(ROW_TILE // ATT_SUB) for hh in range(HEADS_PER_PAIR)]
        n_units = len(units)
        s, p, outs = {}, {}, []
        for step in range(n_units + 2):
            if step < n_units:
                s[step] = scores(*units[step])
            if step >= 2:
                pair, j, hh = units[step - 2]
                of = _dot(p.pop(step - 2), vh[pair][hh][window(j)])
                outs.append(of / pltpu.roll(of, ATT_DH, axis=1))
                if hh == HEADS_PER_PAIR - 1:
                    rows = slice(j * ATT_SUB, (j + 1) * ATT_SUB)
                    y_ref[rows, pair * LANES:(pair + 1) * LANES] = jnp.where(
                        lane_head0, outs[0], outs[1]).astype(BF16)
                    outs = []
            if 1 <= step <= n_units:
                su = s.pop(step - 1)
                m = jnp.max(su, axis=-1, keepdims=True)
                p[step - 1] = jnp.exp2(su - m).astype(BF16)

    @pl.when(pl.program_id(2) == 0)
    def _():
        body(True)

    @pl.when(pl.program_id(2) > 0)
    def _():
        body(False)


def _attention(att_h, diag, batch):
    n = att_h.shape[0]
    seq = n // batch
    att3 = att_h.reshape(batch, seq, ATT_COLS)
    groups = ATT_WIDTH // (ATT_PAIRS * LANES)
    width = ATT_PAIRS * LANES
    head0 = jnp.asarray(
        np.broadcast_to(np.arange(LANES) < ATT_DH, (WIN_ROWS, LANES)).astype(np.float32), BF16)
    left = ATT_LEFT_CHUNKS * CHUNK
    window = lambda off: pl.BlockSpec(
        (None, pl.Element(WIN_ROWS), pl.Element(width)),
        lambda g, b, i: (b, jnp.maximum(i * (ROW_TILE // left) - 1, 0) * left,
                         (off + g * ATT_PAIRS) * LANES))
    y = pl.pallas_call(
        _att_kernel,
        grid=(groups, batch, seq // ROW_TILE),
        in_specs=[pl.BlockSpec((None, ROW_TILE, width), lambda g, b, i: (b, i, g)),
                  window(ATT_WIDTH // LANES), window(2 * ATT_WIDTH // LANES),
                  pl.BlockSpec((None, ATT_PAIRS * HEADS_PER_PAIR, ATT_DIAG),
                               lambda g, b, i: (g, 0, 0)),
                  pl.BlockSpec((WIN_ROWS, LANES), lambda g, b, i: (0, 0))],
        out_specs=pl.BlockSpec((None, ROW_TILE, width), lambda g, b, i: (b, i, g)),
        out_shape=jax.ShapeDtypeStruct((batch, seq, ATT_WIDTH), BF16),
        scratch_shapes=[pltpu.VMEM((ATT_PAIRS * HEADS_PER_PAIR, ATT_SUB, ATT_WIN), F32)],
        compiler_params=pltpu.CompilerParams(
            dimension_semantics=("arbitrary", "arbitrary", "arbitrary"),
            vmem_limit_bytes=VMEM_LIMIT_BYTES),
        name="band_attention",
    )(att3, att3, att3, diag, head0)
    return y.reshape(n, ATT_WIDTH)


def _attention_diag(rel_bias):
    n_diag = ATT_SUB + ATT_WIN - 1
    rel = ATT_LEFT_CHUNKS * CHUNK + (ATT_SUB - 1) - np.arange(n_diag)
    idx = np.clip(rel, -REL_CLIP, REL_CLIP) + REL_CLIP
    diag = jnp.pad(rel_bias.astype(F32)[:, idx] * LOG2_E, ((0, 0), (0, ATT_DIAG - n_diag)))
    return diag.reshape(-1, ATT_PAIRS * HEADS_PER_PAIR, ATT_DIAG)


def _post_kernel(yg_ref, ya_ref, yl_ref, x_ref, p_ref, wo_ref, ln_ref,
                 wg_ref, wu_ref, wd_ref, wpg_ref, wpp_ref, o_ref, act_ref):
    g1_ref, b1_ref, g2_ref, b2_ref = (_Rows(ln_ref, r, r + 1) for r in range(4))
    n_parts = POST_TILE // POST_ROWS
    rows = [slice(k * POST_ROWS, (k + 1) * POST_ROWS) for k in range(n_parts)]
    n_ff = D_FF // FF_TILE
    mix, x1, xb, ffn, ple = ({} for _ in range(5))

    def out_projection(k):
        mix[k] = (_dot(yg_ref[rows[k], :], wo_ref[0:GLA_WIDTH, :])
                  + _dot(ya_ref[rows[k], :], wo_ref[GLA_WIDTH:GLA_WIDTH + ATT_WIDTH, :])
                  + _dot(yl_ref[rows[k], :], wo_ref[GLA_WIDTH + ATT_WIDTH:D_MODEL, :]))

    def norm1(k):
        x1[k] = _layer_norm(DN_ALPHA * x_ref[rows[k], :] + mix.pop(k), g1_ref[...], b1_ref[...])
        xb[k] = x1[k].astype(BF16)

    def ffn_up(k, c):
        cols = slice(c * FF_TILE, (c + 1) * FF_TILE)
        gate = _dot(xb[k], wg_ref[:, cols])
        up = _dot(xb[k], wu_ref[:, cols])
        act_ref[k % 2, :, cols] = (gate * jax.nn.sigmoid(gate) * up).astype(BF16)

    def ffn_down(k):
        ffn[k] = _dot(act_ref[k % 2], wd_ref[...])
        ple[k] = (jax.nn.sigmoid(_dot(xb.pop(k), wpg_ref[...]))
                  * _dot(p_ref[rows[k], :].astype(BF16), wpp_ref[...]))

    def norm2(k):
        o_ref[rows[k], :] = _layer_norm(DN_ALPHA * x1.pop(k) + ffn.pop(k) + ple.pop(k),
                                        g2_ref[...], b2_ref[...])

    out_projection(0)
    norm1(0)
    for k in range(n_parts + 2):
        if k < n_parts:
            if k + 1 < n_parts:
                out_projection(k + 1)
            for c in range(n_ff):
                ffn_up(k, c)
                if c == n_ff // 2 and k + 1 < n_parts:
                    norm1(k + 1)
        if 1 <= k <= n_parts:
            ffn_down(k - 1)
        if k >= 2:
            norm2(k - 2)


def _post(layer, y_gla, y_att, y_lru, x2, p3, wo, ln, wg, wu, wd, wpg, wpp):
    n = x2.shape[0]
    row = lambda c: pl.BlockSpec((POST_TILE, c), lambda i: (i, 0))
    full = lambda r, c: pl.BlockSpec((None, r, c), lambda i: (layer, 0, 0),
                                     pipeline_mode=pl.Buffered(1))
    return pl.pallas_call(
        _post_kernel,
        grid=(n // POST_TILE,),
        in_specs=[row(GLA_WIDTH), row(ATT_WIDTH), row(LRU_WIDTH), row(D_MODEL),
                  pl.BlockSpec((None, POST_TILE, PLE_DIM), lambda i: (layer, i, 0)),
                  full(D_MODEL, D_MODEL), full(4, D_MODEL),
                  full(D_MODEL, D_FF), full(D_MODEL, D_FF), full(D_FF, D_MODEL),
                  full(D_MODEL, D_MODEL), full(PLE_DIM, D_MODEL)],
        out_specs=row(D_MODEL),
        out_shape=jax.ShapeDtypeStruct((n, D_MODEL), F32),
        scratch_shapes=[pltpu.VMEM((2, POST_ROWS, D_FF), BF16)],
        compiler_params=pltpu.CompilerParams(
            dimension_semantics=("arbitrary",), vmem_limit_bytes=VMEM_LIMIT_BYTES),
        name="out_proj_ffn",
    )(y_gla, y_att, y_lru, x2, p3, wo, ln, wg, wu, wd, wpg, wpp)


def _block_diag(w):
    n_layers, g, c, d = w.shape
    eye = jnp.eye(g, dtype=w.dtype)
    return (eye[None, :, None, :, None] * w[:, :, :, None, :]).reshape(n_layers, g * c, g * d)


def _permute_w_in(w):
    wt = jnp.swapaxes(w, 1, 2)
    glr_lo = GLA_COLS
    glr_hi = GLA_COLS + GLA_GATE_RANK
    aq_hi = glr_hi + ATT_WIDTH
    pad = jnp.zeros((wt.shape[0], GLR_COLS - GLA_GATE_RANK, wt.shape[2]), w.dtype)
    return jnp.concatenate(
        [wt[:, :glr_lo], wt[:, glr_hi:aq_hi] * (ATT_DH ** -0.5 * LOG2_E), wt[:, aq_hi:],
         wt[:, glr_lo:glr_hi], pad], axis=1).astype(BF16)


def kernel(x, p, w_in, gla_w_lr_up, gla_b_lr_up, gla_norm_g, rel_bias, lru_conv_w, lru_conv_b,
           lru_w_a, lru_b_a, lru_w_i, lru_b_i, lru_lambda, w_out, ln1_g, ln1_b, w_ffn_gate,
           w_ffn_up, w_ffn_down, w_ple_gate, w_ple_proj, ln2_g, ln2_b):
    batch, seq, _ = x.shape
    n = batch * seq
    assert seq % ROW_TILE == 0
    x2 = x.reshape(n, D_MODEL)
    att_diag = _attention_diag(rel_bias)
    bf16 = lambda a: a.astype(BF16)

    mixer_params = (
        _permute_w_in(w_in),
        bf16(jnp.pad(gla_w_lr_up, ((0, 0), (0, GLR_COLS - GLA_GATE_RANK), (0, 0)))),
        _mixer_vectors(gla_b_lr_up, gla_norm_g, lru_conv_b, lru_b_a, lru_b_i, lru_lambda,
                       lru_conv_w),
        bf16(_block_diag(lru_w_a)), bf16(_block_diag(lru_w_i)))
    post_params = (
        bf16(w_out), jnp.stack([ln1_g, ln1_b, ln2_g, ln2_b], axis=1).astype(F32),
        bf16(w_ffn_gate), bf16(w_ffn_up), bf16(w_ffn_down), bf16(w_ple_gate), bf16(w_ple_proj))
    p3 = p.reshape(DEPTH, n, PLE_DIM)

    for layer in range(DEPTH):
        att_h, y_gla, y_lru = _mixer_in(layer, x2, *mixer_params, batch)
        y_att = _attention(att_h, att_diag, batch)
        x2 = _post(layer, y_gla, y_att, y_lru, x2, p3, *post_params)
    return x2.reshape(batch, seq, D_MODEL)
```

```python
import itertools
import math

import numpy as np
import jax
import jax.numpy as jnp
from jax import lax
from jax.experimental import pallas as pl
from jax.experimental.pallas import tpu as pltpu

F32 = jnp.float32
BF16 = jnp.bfloat16

D_MODEL = 1024
CHUNK = 64
GLA_HEADS = 4
GLA_DK = 64
GLA_WIDTH = 256
GLA_GATE_RANK = 16
GLA_GATE_TAU = 16.0
ATT_HEADS = 8
ATT_DH = 64
ATT_WIDTH = 512
ATT_LEFT_CHUNKS = 8
REL_CLIP = 256
LRU_WIDTH = 256
LRU_BLOCKS = 4
LRU_C = 8.0
CONV_WIDTH = 4
D_FF = 2816
PLE_DIM = 256
DEPTH = 2
DN_ALPHA = (2 * DEPTH) ** 0.25
LN_EPS = 1e-5
RMS_EPS = 1e-6
NEG_INF = -1e30
LOG2_E = math.log2(math.e)

LANES = 128
SUBLANES = 8
VMEM_LIMIT_BYTES = 56 * 1024 * 1024

GLA_COLS = 4 * GLA_WIDTH
ATT_COLS = 3 * ATT_WIDTH
LRU_COLS = 2 * LRU_WIDTH
GLR_COLS = LANES
ATT_OFF = GLA_COLS
LRU_OFF = ATT_OFF + ATT_COLS
GLR_OFF = LRU_OFF + LRU_COLS
D_IN_PAD = GLR_OFF + GLR_COLS
D_IN = GLA_COLS + GLA_GATE_RANK + ATT_COLS + LRU_COLS

ROW_TILE = 1024
MIX_TILE = 512
ATT_SUB = 256
ATT_WIN = ATT_SUB + ATT_LEFT_CHUNKS * CHUNK
WIN_ROWS = ROW_TILE + ATT_LEFT_CHUNKS * CHUNK
ATT_DIAG = ATT_SUB + ATT_WIN
HEADS_PER_PAIR = LANES // ATT_DH
ATT_PAIRS = 4
FF_TILE = 256
PROJ_PIECE = 512
POST_TILE = 1024
POST_ROWS = 256


def _dot(a, b):
    return jnp.dot(a, b, preferred_element_type=F32)


def _dot_nt(a, b):
    return lax.dot_general(a, b, (((1,), (1,)), ((), ())), preferred_element_type=F32)


def _dot_tn(a, b):
    return lax.dot_general(a, b, (((0,), (0,)), ((), ())), preferred_element_type=F32)


def _block_diag_mask(n, blk):
    r = lax.broadcasted_iota(jnp.int32, (n, n), 0) // blk
    c = lax.broadcasted_iota(jnp.int32, (n, n), 1) // blk
    return r == c


def _layer_norm(z, g, b):
    mu = jnp.mean(z, axis=-1, keepdims=True)
    zc = z - mu
    var = jnp.mean(zc * zc, axis=-1, keepdims=True)
    return zc * lax.rsqrt(var + LN_EPS) * g + b


def _softplus(x):
    return jnp.maximum(x, 0.0) + jnp.log1p(jnp.exp(-jnp.abs(x)))


class _Rows:
    def __init__(self, ref, lo, hi):
        self.ref, self.lo, self.hi = ref, lo, hi

    def __getitem__(self, idx):
        if idx is Ellipsis:
            return self.ref[self.lo:self.hi, :]
        rows, cols = idx
        return self.ref[self.lo + rows.start:self.lo + rows.stop, cols]


def _shift_rows(x, d, fill, row_idx):
    return jnp.where(row_idx >= d, pltpu.roll(x, d, axis=0), fill)


def _gla_rows(get_h, wup_ref, bup_ref, ng_ref, bdm_ref, o_ref, state_ref, y_ref):
    h, lr = get_h()
    t = h.shape[0]
    z = _dot(lr.astype(BF16), wup_ref[...]) + bup_ref[...]
    la = -_softplus(-z) * (1.0 / GLA_GATE_TAU)
    la_hi = la.astype(BF16)
    la_lo = (la - la_hi.astype(F32)).astype(BF16)
    ri = lax.broadcasted_iota(jnp.int32, (CHUNK, CHUNK), 0)
    ci = lax.broadcasted_iota(jnp.int32, (CHUNK, CHUNK), 1)
    tri = jnp.where(ci <= ri, 1.0, 0.0).astype(BF16)
    bcum, refb, lastb = [], [], []
    for c in range(t // CHUNK):
        rows = slice(c * CHUNK, (c + 1) * CHUNK)
        bc = _dot(tri, la_hi[rows]) + _dot(tri, la_lo[rows])
        bcum.append(bc)
        refb.append(jnp.broadcast_to(bc[CHUNK // 2:CHUNK // 2 + 1], bc.shape))
        lastb.append(jnp.broadcast_to(bc[CHUNK - 1:CHUNK], bc.shape))
    bcum = jnp.concatenate(bcum, axis=0)
    refb = jnp.concatenate(refb, axis=0)
    lastb = jnp.concatenate(lastb, axis=0)
    yield

    q = h[:, 0:GLA_WIDTH] * (GLA_DK ** -0.5)
    k = h[:, GLA_WIDTH:2 * GLA_WIDTH]
    qe = (q * jnp.exp(bcum - refb)).astype(BF16)
    ke = (k * jnp.exp(refb - bcum)).astype(BF16)
    qd = (q * jnp.exp(bcum)).astype(BF16)
    kd = (k * jnp.exp(lastb - bcum)).astype(BF16)
    vb = h[:, 2 * GLA_WIDTH:3 * GLA_WIDTH].astype(BF16)
    yield

    bd = _block_diag_mask(GLA_WIDTH, GLA_DK)
    bd16 = bdm_ref[...] > 0
    zero16 = jnp.zeros((GLA_WIDTH, GLA_WIDTH), BF16)
    ri4 = lax.broadcasted_iota(jnp.int32, (CHUNK, GLA_WIDTH), 0)
    ci4 = lax.broadcasted_iota(jnp.int32, (CHUNK, GLA_WIDTH), 1) % CHUNK
    causal4 = ci4 <= ri4

    state = state_ref[...]
    for c in range(t // CHUNK):
        rows = slice(c * CHUNK, (c + 1) * CHUNK)
        kstack = jnp.where(bd16, jnp.concatenate([ke[rows]] * GLA_HEADS, axis=0), zero16)
        vstack = jnp.where(bd16, jnp.concatenate([vb[rows]] * GLA_HEADS, axis=0), zero16)
        attn = _dot_nt(qe[rows], kstack)
        attn = jnp.where(causal4, attn, 0.0).astype(BF16)
        o_ref[rows, :] = _dot(attn, vstack) + _dot_nt(qd[rows], state.astype(BF16))
        ut = _dot_tn(vb[rows], kd[rows])
        decay = jnp.exp(lastb[c * CHUNK:c * CHUNK + 1, :])
        state = state * decay + jnp.where(bd, ut, 0.0)
        if c % 2 == 1:
            yield
    state_ref[...] = state

    o = o_ref[...]
    group_mean = jnp.where(bd16, jnp.full_like(zero16, 1.0 / GLA_DK), zero16)
    ms = _dot((o * o).astype(BF16), group_mean)
    g = h[:, 3 * GLA_WIDTH:4 * GLA_WIDTH]
    y_ref[...] = (o * lax.rsqrt(ms + RMS_EPS) * ng_ref[...] * (g * jax.nn.sigmoid(g))).astype(BF16)


def _lru_rows(get_h, cw_ref, cb_ref, wa_ref, ba_ref, wi_ref, bi_ref, lam_ref,
              xtail, hcarry, xbuf, a_scr, u_scr, h_scr, hin_scr, y_ref):
    h = get_h()
    t = h.shape[0]
    n_groups = t // SUBLANES
    xbuf[0:SUBLANES, :] = xtail[...]
    xbuf[SUBLANES:SUBLANES + t, :] = h[:, 0:LRU_WIDTH]
    xc = cb_ref[...] + jnp.zeros((t, LRU_WIDTH), F32)
    for w in range(CONV_WIDTH):
        off = SUBLANES - (CONV_WIDTH - 1) + w
        xc = xc + cw_ref[w:w + 1, :] * xbuf[off:off + t, :]
    xtail[...] = xbuf[t:t + SUBLANES, :]
    yield

    xcb = xc.astype(BF16)
    r = jax.nn.sigmoid(_dot(xcb, wa_ref[...]) + ba_ref[...])
    ig = jax.nn.sigmoid(_dot(xcb, wi_ref[...]) + bi_ref[...])
    log_a = (-LRU_C) * r * _softplus(-lam_ref[...])
    a = jnp.exp(log_a)
    th = jnp.tanh(log_a)
    u = jnp.sqrt(-2.0 * th / (1.0 - th)) * (ig * xc)
    yield

    row_in_group = lax.broadcasted_iota(jnp.int32, (t, LRU_WIDTH), 0) % SUBLANES
    d = 1
    while d < SUBLANES:
        u = u + a * _shift_rows(u, d, 0.0, row_in_group)
        a = a * _shift_rows(a, d, 1.0, row_in_group)
        d *= 2
    lane_halves = [slice(v * LANES, (v + 1) * LANES) for v in range(LRU_WIDTH // LANES)]
    group_last = pl.ds(SUBLANES - 1, n_groups, stride=SUBLANES)
    for v, lanes in enumerate(lane_halves):
        a_scr[v] = a[:, lanes]
        u_scr[v] = u[:, lanes]
    yield
    a_g = jnp.concatenate([a_scr[v, group_last, :] for v in range(len(lane_halves))], axis=1)
    u_g = jnp.concatenate([u_scr[v, group_last, :] for v in range(len(lane_halves))], axis=1)
    group_idx = lax.broadcasted_iota(jnp.int32, (n_groups, LRU_WIDTH), 0)
    d = 1
    while d < n_groups:
        u_g = u_g + a_g * _shift_rows(u_g, d, 0.0, group_idx)
        a_g = a_g * _shift_rows(a_g, d, 1.0, group_idx)
        d *= 2
    h_prev = hcarry[0:1, :]
    h_end = u_g + a_g * h_prev
    hin_scr[...] = _shift_rows(h_end, 1, h_prev, group_idx)
    hcarry[...] = jnp.broadcast_to(h_end[n_groups - 1:n_groups, :], hcarry.shape)
    yield
    for g in range(n_groups):
        rows = slice(g * SUBLANES, (g + 1) * SUBLANES)
        for v, lanes in enumerate(lane_halves):
            h_scr[rows, lanes] = u_scr[v, rows, :] + a_scr[v, rows, :] * hin_scr[g:g + 1, lanes]
    y_ref[...] = (h_scr[...] * jax.nn.gelu(h[:, LRU_WIDTH:2 * LRU_WIDTH])).astype(BF16)


def _round_robin(*gens):
    pending = list(gens)
    while pending:
        alive = []
        for g in pending:
            if next(g, StopIteration) is not StopIteration:
                alive.append(g)
                yield
        pending = alive


def _mixer_in_kernel(x_ref, wt_ref, wup_ref, vec_ref, bdm_ref, wa_ref, wi_ref,
                     att_ref, yg_ref, yl_ref,
                     w_ref, state_scr, xtail, hcarry, o_scr, xbuf, a_scr, u_scr, h_scr, hin_scr):
    @pl.when((pl.program_id(0) == 0) & (pl.program_id(1) == 0))
    def _():
        for c in range(0, GLR_OFF, LANES):
            src = c if c < GLA_COLS else c + GLA_GATE_RANK
            piece = wt_ref[src:src + LANES, :]
            if GLA_COLS <= c < GLA_COLS + ATT_WIDTH:
                piece = piece * (ATT_DH ** -0.5 * LOG2_E)
            w_ref[:, c:c + LANES] = piece.astype(BF16).T
        rank = jnp.concatenate([wt_ref[GLA_COLS:GLA_COLS + GLA_GATE_RANK, :],
                                jnp.zeros((GLR_COLS - GLA_GATE_RANK, D_MODEL), F32)], axis=0)
        w_ref[:, GLR_OFF:GLR_OFF + GLR_COLS] = rank.astype(BF16).T

    @pl.when(pl.program_id(1) == 0)
    def _():
        state_scr[...] = jnp.zeros_like(state_scr)
        xtail[...] = jnp.zeros_like(xtail)
        hcarry[...] = jnp.zeros_like(hcarry)

    bup_ref, ng_ref, cb_ref, ba_ref, bi_ref, lam_ref = (_Rows(vec_ref, r, r + 1) for r in range(6))
    cw_ref = _Rows(vec_ref, 6, 6 + CONV_WIDTH)

    xb = x_ref[...].astype(BF16)
    out = {}

    def project(name, off, width):
        cols = []
        for c in range(0, width, PROJ_PIECE):
            n_cols = min(PROJ_PIECE, width - c)
            cols.append(_dot(xb, w_ref[:, off + c:off + c + n_cols]))
            if c + n_cols == width:
                out[name] = jnp.concatenate(cols, axis=1) if len(cols) > 1 else cols[0]
            yield

    def project_attention():
        for c in range(0, ATT_COLS, PROJ_PIECE):
            att_ref[:, c:c + PROJ_PIECE] = _dot(
                xb, w_ref[:, ATT_OFF + c:ATT_OFF + c + PROJ_PIECE]).astype(BF16)
            yield

    lru = _lru_rows(lambda: out["lru"], cw_ref, cb_ref, wa_ref, ba_ref, wi_ref, bi_ref, lam_ref,
                    xtail, hcarry, xbuf, a_scr, u_scr, h_scr, hin_scr, yl_ref)
    gla = _gla_rows(lambda: (out["gla"], out["glr"]), wup_ref, bup_ref, ng_ref, bdm_ref,
                    o_scr, state_scr, yg_ref)
    stages = (project("lru", LRU_OFF, LRU_COLS),
              _round_robin(itertools.chain(project("glr", GLR_OFF, GLR_COLS),
                                           project("gla", 0, GLA_COLS)), lru),
              _round_robin(project_attention(), gla))
    for _ in itertools.chain(*stages):
        pass


def _mixer_vectors(gla_b_lr_up, gla_norm_g, lru_conv_b, lru_b_a, lru_b_i, lru_lambda, lru_conv_w):
    rows = jnp.stack([gla_b_lr_up, jnp.tile(gla_norm_g, (1, GLA_HEADS)), lru_conv_b, lru_b_a,
                      lru_b_i, lru_lambda], axis=1)
    return jnp.concatenate([rows, lru_conv_w], axis=1).astype(F32)


def _mixer_in(layer, x2, w_perm, wup_pad, vectors, wa_bd, wi_bd, batch):
    n = x2.shape[0]
    nt = n // batch // MIX_TILE
    d = np.arange(GLA_WIDTH) // GLA_DK
    bdm = jnp.asarray((d[:, None] == d[None, :]).astype(np.float32), BF16)
    row = lambda c: pl.BlockSpec((MIX_TILE, c), lambda b, i: (b * nt + i, 0))
    full = lambda r, c: pl.BlockSpec((r, c), lambda b, i: (0, 0))
    of_layer = lambda r, c: pl.BlockSpec((None, r, c), lambda b, i: (layer, 0, 0))
    tile = lambda r, c: pltpu.VMEM((r, c), F32)
    lane_halves = pltpu.VMEM((LRU_WIDTH // LANES, MIX_TILE, LANES), F32)
    return pl.pallas_call(
        _mixer_in_kernel,
        grid=(batch, nt),
        in_specs=[row(D_MODEL),
                  pl.BlockSpec((None, D_IN, D_MODEL), lambda b, i: (layer, 0, 0),
                               pipeline_mode=pl.Buffered(1)),
                  of_layer(GLR_COLS, GLA_WIDTH), of_layer(6 + CONV_WIDTH, GLA_WIDTH),
                  full(GLA_WIDTH, GLA_WIDTH),
                  of_layer(LRU_WIDTH, LRU_WIDTH), of_layer(LRU_WIDTH, LRU_WIDTH)],
        out_specs=[row(ATT_COLS), row(GLA_WIDTH), row(LRU_WIDTH)],
        out_shape=[jax.ShapeDtypeStruct((n, ATT_COLS), BF16),
                   jax.ShapeDtypeStruct((n, GLA_WIDTH), BF16),
                   jax.ShapeDtypeStruct((n, LRU_WIDTH), BF16)],
        scratch_shapes=[pltpu.VMEM((D_MODEL, D_IN_PAD), BF16),
                        pltpu.VMEM((GLA_WIDTH, GLA_WIDTH), F32),
                        pltpu.VMEM((SUBLANES, LRU_WIDTH), F32),
                        pltpu.VMEM((SUBLANES, LRU_WIDTH), F32),
                        tile(MIX_TILE, GLA_WIDTH),
                        tile(MIX_TILE + SUBLANES, LRU_WIDTH),
                        lane_halves, lane_halves,
                        tile(MIX_TILE, LRU_WIDTH),
                        tile(MIX_TILE // SUBLANES, LRU_WIDTH)],
        compiler_params=pltpu.CompilerParams(
            dimension_semantics=("arbitrary", "arbitrary"), vmem_limit_bytes=VMEM_LIMIT_BYTES),
        name="mixer_in",
    )(x2, w_perm, wup_pad, vectors, bdm, wa_bd, wi_bd)


def _att_kernel(q_ref, kw_ref, vw_ref, diag_ref, hm_ref, y_ref, bias_ref):
    lane_head0 = lax.broadcasted_iota(jnp.int32, (ATT_SUB, LANES), 1) < ATT_DH

    @pl.when((pl.program_id(1) == 0) & (pl.program_id(2) == 0))
    def _():
        r = lax.broadcasted_iota(jnp.int32, (ATT_SUB, ATT_WIN), 0)
        w = lax.broadcasted_iota(jnp.int32, (ATT_SUB, ATT_WIN), 1)
        kk = w - (r // CHUNK) * CHUNK
        in_band = (kk >= 0) & (kk < (ATT_LEFT_CHUNKS + 1) * CHUNK)
        for hh in range(ATT_PAIRS * HEADS_PER_PAIR):
            rows = jnp.broadcast_to(diag_ref[hh:hh + 1, :], (ATT_SUB, ATT_DIAG))
            skew = pltpu.roll(rows, ATT_DIAG - (ATT_SUB - 1), axis=1, stride=1, stride_axis=0)
            bias_ref[hh] = jnp.where(in_band, skew[:, 0:ATT_WIN], NEG_INF)

    def body(first):
        n_keys = ROW_TILE if first else WIN_ROWS
        head0 = hm_ref[0:n_keys, :] > 0
        vh, qh = [], []
        for pair in range(ATT_PAIRS):
            lanes = slice(pair * LANES, (pair + 1) * LANES)
            v = vw_ref[0:n_keys, lanes]
            one16 = jnp.ones_like(v)
            vh.append((jnp.where(head0, v, one16), jnp.where(head0, one16, v)))
            q = q_ref[:, lanes]
            zero16 = jnp.zeros_like(q)
            qh.append((jnp.where(head0[0:ROW_TILE], q, zero16),
                       jnp.where(head0[0:ROW_TILE], zero16, q)))

        def window(j):
            if first:
                return slice(max(0, (j + 1) * ATT_SUB - ATT_WIN), (j + 1) * ATT_SUB)
            return slice(j * ATT_SUB, j * ATT_SUB + ATT_WIN)

        def scores(pair, j, hh):
            win = window(j)
            n_win = win.stop - win.start
            rows = slice(j * ATT_SUB, (j + 1) * ATT_SUB)
            return (_dot_nt(qh[pair][hh][rows], kw_ref[win, pair * LANES:(pair + 1) * LANES])
                    + bias_ref[pair * HEADS_PER_PAIR + hh, :, ATT_WIN - n_win:ATT_WIN])

        units = [(pair, j, hh) for pair in range(ATT_PAIRS)
                 for j in range(ROW_TILE // ATT_SUB) for hh in range(HEADS_PER_PAIR)]
        n_units = len(units)
        s, p, outs = {}, {}, []
        for step in range(n_units + 2):
            if step < n_units:
                s[step] = scores(*units[step])
            if step >= 2:
                pair, j, hh = units[step - 2]
                of = _dot(p.pop(step - 2), vh[pair][hh][window(j)])
                outs.append(of / pltpu.roll(of, ATT_DH, axis=1))
                if hh == HEADS_PER_PAIR - 1:
                    rows = slice(j * ATT_SUB, (j + 1) * ATT_SUB)
                    y_ref[rows, pair * LANES:(pair + 1) * LANES] = jnp.where(
                        lane_head0, outs[0], outs[1]).astype(BF16)
                    outs = []
            if 1 <= step <= n_units:
                su = s.pop(step - 1)
                m = jnp.max(su, axis=-1, keepdims=True)
                p[step - 1] = jnp.exp2(su - m).astype(BF16)

    @pl.when(pl.program_id(2) == 0)
    def _():
        body(True)

    @pl.when(pl.program_id(2) > 0)
    def _():
        body(False)


def _attention(att_h, diag, batch):
    n = att_h.shape[0]
    seq = n // batch
    att3 = att_h.reshape(batch, seq, ATT_COLS)
    groups = ATT_WIDTH // (ATT_PAIRS * LANES)
    width = ATT_PAIRS * LANES
    head0 = jnp.asarray(
        np.broadcast_to(np.arange(LANES) < ATT_DH, (WIN_ROWS, LANES)).astype(np.float32), BF16)
    left = ATT_LEFT_CHUNKS * CHUNK
    window = lambda off: pl.BlockSpec(
        (None, pl.Element(WIN_ROWS), pl.Element(width)),
        lambda g, b, i: (b, jnp.maximum(i * (ROW_TILE // left) - 1, 0) * left,
                         (off + g * ATT_PAIRS) * LANES))
    y = pl.pallas_call(
        _att_kernel,
        grid=(groups, batch, seq // ROW_TILE),
        in_specs=[pl.BlockSpec((None, ROW_TILE, width), lambda g, b, i: (b, i, g)),
                  window(ATT_WIDTH // LANES), window(2 * ATT_WIDTH // LANES),
                  pl.BlockSpec((None, ATT_PAIRS * HEADS_PER_PAIR, ATT_DIAG),
                               lambda g, b, i: (g, 0, 0)),
                  pl.BlockSpec((WIN_ROWS, LANES), lambda g, b, i: (0, 0))],
        out_specs=pl.BlockSpec((None, ROW_TILE, width), lambda g, b, i: (b, i, g)),
        out_shape=jax.ShapeDtypeStruct((batch, seq, ATT_WIDTH), BF16),
        scratch_shapes=[pltpu.VMEM((ATT_PAIRS * HEADS_PER_PAIR, ATT_SUB, ATT_WIN), F32)],
        compiler_params=pltpu.CompilerParams(
            dimension_semantics=("arbitrary", "arbitrary", "arbitrary"),
            vmem_limit_bytes=VMEM_LIMIT_BYTES),
        name="band_attention",
    )(att3, att3, att3, diag, head0)
    return y.reshape(n, ATT_WIDTH)


def _attention_diag(rel_bias):
    n_diag = ATT_SUB + ATT_WIN - 1
    rel = ATT_LEFT_CHUNKS * CHUNK + (ATT_SUB - 1) - np.arange(n_diag)
    idx = np.clip(rel, -REL_CLIP, REL_CLIP) + REL_CLIP
    diag = jnp.pad(rel_bias.astype(F32)[:, idx] * LOG2_E, ((0, 0), (0, ATT_DIAG - n_diag)))
    return diag.reshape(-1, ATT_PAIRS * HEADS_PER_PAIR, ATT_DIAG)


def _post_kernel(yg_ref, ya_ref, yl_ref, x_ref, p_ref, wo_ref, ln_ref,
                 wg_ref, wu_ref, wd_ref, wpg_ref, wpp_ref, o_ref, act_ref):
    g1_ref, b1_ref, g2_ref, b2_ref = (_Rows(ln_ref, r, r + 1) for r in range(4))
    n_parts = POST_TILE // POST_ROWS
    rows = [slice(k * POST_ROWS, (k + 1) * POST_ROWS) for k in range(n_parts)]
    n_ff = D_FF // FF_TILE
    mix, x1, xb, ffn, ple = ({} for _ in range(5))

    def out_projection(k):
        mix[k] = (_dot(yg_ref[rows[k], :], wo_ref[0:GLA_WIDTH, :])
                  + _dot(ya_ref[rows[k], :], wo_ref[GLA_WIDTH:GLA_WIDTH + ATT_WIDTH, :])
                  + _dot(yl_ref[rows[k], :], wo_ref[GLA_WIDTH + ATT_WIDTH:D_MODEL, :]))

    def norm1(k):
        x1[k] = _layer_norm(DN_ALPHA * x_ref[rows[k], :] + mix.pop(k), g1_ref[...], b1_ref[...])
        xb[k] = x1[k].astype(BF16)

    def ffn_up(k, c):
        cols = slice(c * FF_TILE, (c + 1) * FF_TILE)
        gate = _dot(xb[k], wg_ref[:, cols])
        up = _dot(xb[k], wu_ref[:, cols])
        act_ref[k % 2, :, cols] = (gate * jax.nn.sigmoid(gate) * up).astype(BF16)

    def ffn_down(k):
        ffn[k] = _dot(act_ref[k % 2], wd_ref[...])
        ple[k] = (jax.nn.sigmoid(_dot(xb.pop(k), wpg_ref[...]))
                  * _dot(p_ref[rows[k], :].astype(BF16), wpp_ref[...]))

    def norm2(k):
        o_ref[rows[k], :] = _layer_norm(DN_ALPHA * x1.pop(k) + ffn.pop(k) + ple.pop(k),
                                        g2_ref[...], b2_ref[...])

    out_projection(0)
    norm1(0)
    for k in range(n_parts + 2):
        if k < n_parts:
            if k + 1 < n_parts:
                out_projection(k + 1)
            for c in range(n_ff):
                ffn_up(k, c)
                if c == n_ff // 2 and k + 1 < n_parts:
                    norm1(k + 1)
        if 1 <= k <= n_parts:
            ffn_down(k - 1)
        if k >= 2:
            norm2(k - 2)


def _post(layer, y_gla, y_att, y_lru, x2, p3, wo, ln, wg, wu, wd, wpg, wpp):
    n = x2.shape[0]
    row = lambda c: pl.BlockSpec((POST_TILE, c), lambda i: (i, 0))
    full = lambda r, c: pl.BlockSpec((None, r, c), lambda i: (layer, 0, 0),
                                     pipeline_mode=pl.Buffered(1))
    return pl.pallas_call(
        _post_kernel,
        grid=(n // POST_TILE,),
        in_specs=[row(GLA_WIDTH), row(ATT_WIDTH), row(LRU_WIDTH), row(D_MODEL),
                  pl.BlockSpec((None, POST_TILE, PLE_DIM), lambda i: (layer, i, 0)),
                  full(D_MODEL, D_MODEL), full(4, D_MODEL),
                  full(D_MODEL, D_FF), full(D_MODEL, D_FF), full(D_FF, D_MODEL),
                  full(D_MODEL, D_MODEL), full(PLE_DIM, D_MODEL)],
        out_specs=row(D_MODEL),
        out_shape=jax.ShapeDtypeStruct((n, D_MODEL), F32),
        scratch_shapes=[pltpu.VMEM((2, POST_ROWS, D_FF), BF16)],
        compiler_params=pltpu.CompilerParams(
            dimension_semantics=("arbitrary",), vmem_limit_bytes=VMEM_LIMIT_BYTES),
        name="out_proj_ffn",
    )(y_gla, y_att, y_lru, x2, p3, wo, ln, wg, wu, wd, wpg, wpp)


def _block_diag(w):
    n_layers, g, c, d = w.shape
    eye = jnp.eye(g, dtype=w.dtype)
    return (eye[None, :, None, :, None] * w[:, :, :, None, :]).reshape(n_layers, g * c, g * d)


def kernel(x, p, w_in, gla_w_lr_up, gla_b_lr_up, gla_norm_g, rel_bias, lru_conv_w, lru_conv_b,
           lru_w_a, lru_b_a, lru_w_i, lru_b_i, lru_lambda, w_out, ln1_g, ln1_b, w_ffn_gate,
           w_ffn_up, w_ffn_down, w_ple_gate, w_ple_proj, ln2_g, ln2_b):
    batch, seq, _ = x.shape
    n = batch * seq
    assert seq % ROW_TILE == 0
    x2 = x.reshape(n, D_MODEL)
    att_diag = _attention_diag(rel_bias)
    bf16 = lambda a: a.astype(BF16)

    mixer_params = (
        jnp.swapaxes(w_in, 1, 2).astype(F32),
        bf16(jnp.pad(gla_w_lr_up, ((0, 0), (0, GLR_COLS - GLA_GATE_RANK), (0, 0)))),
        _mixer_vectors(gla_b_lr_up, gla_norm_g, lru_conv_b, lru_b_a, lru_b_i, lru_lambda,
                       lru_conv_w),
        bf16(_block_diag(lru_w_a)), bf16(_block_diag(lru_w_i)))
    post_params = (
        bf16(w_out), jnp.stack([ln1_g, ln1_b, ln2_g, ln2_b], axis=1).astype(F32),
        bf16(w_ffn_gate), bf16(w_ffn_up), bf16(w_ffn_down), bf16(w_ple_gate), bf16(w_ple_proj))
    p3 = p.reshape(DEPTH, n, PLE_DIM)

    for layer in range(DEPTH):
        att_h, y_gla, y_lru = _mixer_in(layer, x2, *mixer_params, batch)
        y_att = _attention(att_h, att_diag, batch)
        x2 = _post(layer, y_gla, y_att, y_lru, x2, p3, *post_params)
    return x2.reshape(batch, seq, D_MODEL)
```
